```python
import math
import jax
import jax.numpy as jnp
from jax import lax
import numpy as np

D_MODEL = 1024
BATCH = 16
SEQ = 2048
DEPTH = 2
DEC_BATCH = 32
DEC_SEQ = 4
PAST_LEN = 16384
PAGE_SIZE = 128

N_HEADS = 8
KV_HEADS = 4
HEAD_DIM = 64
ATTN_DIM = N_HEADS * HEAD_DIM
KV_DIM = KV_HEADS * HEAD_DIM
MOBA_BLOCK = 256
MOBA_TOPK = 3
Q_CHUNK = 128
CONV_DIM = D_MODEL - ATTN_DIM
CONV_WIDTH = 3
MIX_DIM = ATTN_DIM + CONV_DIM
IN_DIM = ATTN_DIM + 2 * KV_DIM + 3 * CONV_DIM
IN_SPLITS = [ATTN_DIM, ATTN_DIM + KV_DIM, ATTN_DIM + 2 * KV_DIM,
             ATTN_DIM + 2 * KV_DIM + CONV_DIM, ATTN_DIM + 2 * KV_DIM + 2 * CONV_DIM]
PEER_HEADS = 8
PEER_KEYS = 128
PEER_EXPERTS = PEER_KEYS * PEER_KEYS
PEER_QDIM = 128
PEER_HALF = PEER_QDIM // 2
PEER_TOPK = 16
PEER_TOKEN_BLOCK = 256
NORM_EPS = 1e-6

kernel_name = 'hymba_moba_shortconv_peer_adaln_step'


def rms_norm(x, g):
    xf = x.astype(jnp.float32)
    y = xf * lax.rsqrt(jnp.mean(xf * xf, axis=-1, keepdims=True) + NORM_EPS)
    return (y * g.astype(jnp.float32)).astype(x.dtype)


def alibi_slopes():
    return jnp.exp2(-8.0 * (jnp.arange(N_HEADS, dtype=jnp.float32) + 1.0) / N_HEADS)


def moba_attention(q, k, v, pos0):
    B, Sq = q.shape[0], q.shape[1]
    L = k.shape[1]
    nb = -(-L // MOBA_BLOCK)
    pad = nb * MOBA_BLOCK - L
    k = jnp.pad(k, ((0, 0), (0, pad), (0, 0), (0, 0)))
    v = jnp.pad(v, ((0, 0), (0, pad), (0, 0), (0, 0)))
    qc = min(Q_CHUNK, Sq)
    nq = Sq // qc
    G = N_HEADS // KV_HEADS
    n_sel = min(MOBA_TOPK, nb)
    n_far = n_sel * MOBA_BLOCK
    slopes = alibi_slopes().reshape(KV_HEADS, G)
    scale = HEAD_DIM ** -0.5
    q_items = q.reshape(B * nq, qc, N_HEADS, HEAD_DIM)
    b_items = jnp.repeat(jnp.arange(B, dtype=jnp.int32), nq)
    t_items = pos0 + jnp.tile(jnp.arange(nq, dtype=jnp.int32), B) * qc
    hidx = jnp.arange(KV_HEADS)[:, None, None, None]
    blk_ids = jnp.arange(nb)
    key_off = jnp.arange(MOBA_BLOCK, dtype=jnp.int32)

    def one(item):
        qi, b, t0 = item
        kb = lax.dynamic_index_in_dim(k, b, 0, keepdims=False).reshape(
            nb, MOBA_BLOCK, KV_HEADS, HEAD_DIM).transpose(2, 0, 1, 3)
        vb = lax.dynamic_index_in_dim(v, b, 0, keepdims=False).reshape(
            nb, MOBA_BLOCK, KV_HEADS, HEAD_DIM).transpose(2, 0, 1, 3)
        qg = qi.reshape(qc, KV_HEADS, G, HEAD_DIM).transpose(1, 2, 0, 3)
        tq = t0 + jnp.arange(qc, dtype=jnp.int32)
        own = t0 // MOBA_BLOCK
        kmean = jnp.mean(kb.astype(jnp.float32), axis=2)
        gate = jnp.einsum('hgqd,hnd->hgqn', qg.astype(jnp.float32), kmean)
        gate = jnp.where(blk_ids < own, gate, -jnp.inf)
        _, sel = lax.top_k(gate, n_sel)
        sel_ok = jnp.arange(n_sel) < own
        k_sel = kb[hidx, sel]
        v_sel = vb[hidx, sel]
        d_sel = (tq[:, None, None] - (sel[..., None] * MOBA_BLOCK + key_off)).astype(jnp.float32)
        s_sel = (jnp.einsum('hgqd,hgqjkd->hgqjk', qg, k_sel).astype(jnp.float32) * scale
                 - slopes[:, :, None, None, None] * d_sel)
        s_sel = jnp.where(sel_ok[:, None], s_sel, -jnp.inf)
        k_own = lax.dynamic_index_in_dim(kb, own, 1, keepdims=False)
        v_own = lax.dynamic_index_in_dim(vb, own, 1, keepdims=False)
        d_own = tq[:, None] - (own * MOBA_BLOCK + key_off)[None, :]
        s_own = (jnp.einsum('hgqd,hkd->hgqk', qg, k_own).astype(jnp.float32) * scale
                 - slopes[:, :, None, None] * d_own.astype(jnp.float32))
        s_own = jnp.where(d_own >= 0, s_own, -jnp.inf)
        scores = jnp.concatenate([s_sel.reshape(KV_HEADS, G, qc, n_far), s_own], axis=-1)
        p = jax.nn.softmax(scores, axis=-1).astype(v.dtype)
        o = (jnp.einsum('hgqjk,hgqjkd->hgqd',
                        p[..., :n_far].reshape(KV_HEADS, G, qc, n_sel, MOBA_BLOCK), v_sel)
             + jnp.einsum('hgqk,hkd->hgqd', p[..., n_far:], v_own))
        return o.transpose(2, 0, 1, 3).reshape(qc, N_HEADS, HEAD_DIM)

    out = lax.map(one, (q_items, b_items, t_items))
    return out.reshape(B, Sq, N_HEADS, HEAD_DIM)


def short_conv(xc, prev, w):
    S = xc.shape[1]
    xp = jnp.concatenate([prev.astype(xc.dtype), xc], axis=1)
    y = w[0] * xp[:, 0:S]
    for j in range(1, CONV_WIDTH):
        y = y + w[j] * xp[:, j:j + S]
    return y, xp[:, -(CONV_WIDTH - 1):]


def peer(h, w_query, sub_keys, u_tab, v_tab):
    Bh, S, D = h.shape
    t = h.reshape(-1, D)
    n = t.shape[0]
    nblk = -(-n // PEER_TOKEN_BLOCK)
    t = jnp.pad(t, ((0, nblk * PEER_TOKEN_BLOCK - n), (0, 0))).reshape(nblk, PEER_TOKEN_BLOCK, D)

    def one(xb):
        T = xb.shape[0]
        q = (xb @ w_query).reshape(T, PEER_HEADS, 2, PEER_HALF)
        s = jnp.einsum('thpd,pkd->thpk', q, sub_keys).astype(jnp.float32)
        s1, i1 = lax.top_k(s[:, :, 0], PEER_TOPK)
        s2, i2 = lax.top_k(s[:, :, 1], PEER_TOPK)
        cand = (s1[..., :, None] + s2[..., None, :]).reshape(T, PEER_HEADS, PEER_TOPK * PEER_TOPK)
        cidx = (i1[..., :, None] * PEER_KEYS + i2[..., None, :]).reshape(T, PEER_HEADS, PEER_TOPK * PEER_TOPK)
        top_s, pos = lax.top_k(cand, PEER_TOPK)
        e = jnp.take_along_axis(cidx, pos, axis=-1).reshape(T, PEER_HEADS * PEER_TOPK)
        g = jax.nn.softmax(top_s, axis=-1).reshape(T, PEER_HEADS * PEER_TOPK).astype(xb.dtype)
        a = jax.nn.gelu(jnp.einsum('td,ted->te', xb, u_tab[e]), approximate=False) * g
        return jnp.einsum('te,ted->td', a, v_tab[e])

    out = lax.map(one, t).reshape(-1, D)[:n]
    return out.reshape(Bh, S, D)


def trunk(x, c, pos0, conv_prev, past, w_ada, b_ada, g_norm1, g_norm2, w_in, conv_w,
          g_out_attn, g_out_conv, w_o, w_query, sub_keys, u_experts, v_experts, g_final):
    B, S, _ = x.shape
    ks, vs, convs = [], [], []
    for l in range(DEPTH):
        mod = (jax.nn.silu(c) @ w_ada[l] + b_ada[l])[:, None, :]
        sh1, sc1, ga1, sh2, sc2, ga2 = jnp.split(mod, 6, axis=-1)
        h = rms_norm(x, g_norm1[l]) * (1 + sc1) + sh1
        q, k, v, gb, gc, xin = jnp.split(h @ w_in[l], IN_SPLITS, axis=-1)
        q = q.reshape(B, S, N_HEADS, HEAD_DIM)
        k = k.reshape(B, S, KV_HEADS, HEAD_DIM)
        v = v.reshape(B, S, KV_HEADS, HEAD_DIM)
        if past is None:
            k_all, v_all = k, v
        else:
            cache_k, cache_v, page_table = past
            n_past = page_table.shape[1] * PAGE_SIZE
            kp = cache_k[l, page_table].reshape(B, n_past, KV_HEADS, HEAD_DIM)
            vp = cache_v[l, page_table].reshape(B, n_past, KV_HEADS, HEAD_DIM)
            k_all = jnp.concatenate([kp.astype(k.dtype), k], axis=1)
            v_all = jnp.concatenate([vp.astype(v.dtype), v], axis=1)
        attn = moba_attention(q, k_all, v_all, pos0).reshape(B, S, ATTN_DIM)
        yc, conv_new = short_conv(gc * xin, conv_prev[l], conv_w[l])
        yc = gb * yc
        mixed = jnp.concatenate([rms_norm(attn, g_out_attn[l]), rms_norm(yc, g_out_conv[l])], axis=-1) @ w_o[l]
        x = x + ga1 * mixed
        h2 = rms_norm(x, g_norm2[l]) * (1 + sc2) + sh2
        x = x + ga2 * peer(h2, w_query[l], sub_keys[l], u_experts[l], v_experts[l])
        ks.append(k)
        vs.append(v)
        convs.append(conv_new)
    return rms_norm(x, g_final), jnp.stack(ks), jnp.stack(vs), jnp.stack(convs)


def setup_inputs(seed: int = 0) -> dict:
    key = jax.random.key(seed)
    ks = jax.random.split(key, 24)
    f32 = jnp.float32
    n_pages = PAST_LEN // PAGE_SIZE
    used = DEC_BATCH * n_pages
    n_pool = used + max(1, used // 4)

    def nrm(k, shape, s=1.0):
        return jax.random.normal(k, shape, f32) * s

    page_table = jax.random.permutation(ks[0], n_pool)[:used].reshape(DEC_BATCH, n_pages).astype(jnp.int32)
    return {
        'x_prompt': nrm(ks[1], (BATCH, SEQ, D_MODEL)),
        'x_sample': nrm(ks[2], (DEC_BATCH, DEC_SEQ, D_MODEL)),
        'cache_k': nrm(ks[3], (DEPTH, n_pool, PAGE_SIZE, KV_HEADS, HEAD_DIM)),
        'cache_v': nrm(ks[4], (DEPTH, n_pool, PAGE_SIZE, KV_HEADS, HEAD_DIM)),
        'state_conv': nrm(ks[5], (DEPTH, DEC_BATCH, CONV_WIDTH - 1, CONV_DIM)),
        'page_table': page_table,
        'c_prompt': nrm(ks[6], (BATCH, D_MODEL)),
        'c_sample': nrm(ks[7], (DEC_BATCH, D_MODEL)),
        'w_ada': nrm(ks[8], (DEPTH, D_MODEL, 6 * D_MODEL), 0.5 * D_MODEL ** -0.5),
        'b_ada': nrm(ks[9], (DEPTH, 6 * D_MODEL), 0.02),
        'g_norm1': 1.0 + nrm(ks[10], (DEPTH, D_MODEL), 0.02),
        'g_norm2': 1.0 + nrm(ks[11], (DEPTH, D_MODEL), 0.02),
        'w_in': nrm(ks[12], (DEPTH, D_MODEL, IN_DIM), D_MODEL ** -0.5),
        'conv_w': nrm(ks[13], (DEPTH, CONV_WIDTH, CONV_DIM), CONV_WIDTH ** -0.5),
        'g_out_attn': 1.0 + nrm(ks[14], (DEPTH, ATTN_DIM), 0.02),
        'g_out_conv': 1.0 + nrm(ks[15], (DEPTH, CONV_DIM), 0.02),
        'w_o': nrm(ks[16], (DEPTH, MIX_DIM, D_MODEL), MIX_DIM ** -0.5),
        'w_query': nrm(ks[17], (DEPTH, D_MODEL, PEER_HEADS * PEER_QDIM), D_MODEL ** -0.5),
        'sub_keys': nrm(ks[18], (DEPTH, 2, PEER_KEYS, PEER_HALF), PEER_HALF ** -0.5),
        'u_experts': nrm(ks[19], (DEPTH, PEER_EXPERTS, D_MODEL), D_MODEL ** -0.5),
        'v_experts': nrm(ks[20], (DEPTH, PEER_EXPERTS, D_MODEL), 0.5),
        'g_final': 1.0 + nrm(ks[21], (D_MODEL,), 0.02),
    }


def reference(x_prompt, x_sample, cache_k, cache_v, state_conv, page_table, c_prompt, c_sample,
              w_ada, b_ada, g_norm1, g_norm2, w_in, conv_w, g_out_attn, g_out_conv, w_o,
              w_query, sub_keys, u_experts, v_experts, g_final):
    zero_conv = jnp.zeros((DEPTH, x_prompt.shape[0], CONV_WIDTH - 1, CONV_DIM), x_prompt.dtype)
    y_prompt, k_prompt, v_prompt, conv_prompt = trunk(
        x_prompt, c_prompt, 0, zero_conv, None,
        w_ada, b_ada, g_norm1, g_norm2, w_in, conv_w, g_out_attn, g_out_conv, w_o,
        w_query, sub_keys, u_experts, v_experts, g_final)
    past_len = page_table.shape[1] * PAGE_SIZE
    y_sample, k_sample, v_sample, conv_sample = trunk(
        x_sample, c_sample, past_len, state_conv, (cache_k, cache_v, page_table),
        w_ada, b_ada, g_norm1, g_norm2, w_in, conv_w, g_out_attn, g_out_conv, w_o,
        w_query, sub_keys, u_experts, v_experts, g_final)
    return (y_prompt, y_sample, k_prompt, v_prompt, conv_prompt, k_sample, v_sample, conv_sample)
```

```python
import functools

import jax
import jax.numpy as jnp
from jax import lax
from jax.experimental import pallas as pl
from jax.experimental.pallas import tpu as pltpu

F32 = jnp.float32
BF16 = jnp.bfloat16

N_HEADS = 8
KV_HEADS = 4
HEAD_DIM = 64
Q_GROUP = N_HEADS // KV_HEADS
ATTN_DIM = N_HEADS * HEAD_DIM
KV_DIM = KV_HEADS * HEAD_DIM
MOBA_BLOCK = 256
MOBA_TOPK = 3
Q_CHUNK = 128
CONV_WIDTH = 3
PEER_HEADS = 8
PEER_KEYS = 128
PEER_HALF = 64
PEER_TOPK = 16
NORM_EPS = 1e-6
NEG_BIG = -1e30
POS_BIG = 1e30

VMEM_LIMIT_BYTES = 56 * 1024 * 1024
LANES = 128

NT_DIMS = (((1,), (1,)), ((), ()))


def _cparams(sem):
    return pltpu.CompilerParams(dimension_semantics=sem, vmem_limit_bytes=VMEM_LIMIT_BYTES)


def _rms(x, g):
    y = x * lax.rsqrt(jnp.mean(x * x, axis=-1, keepdims=True) + NORM_EPS)
    return y * g


def _alibi_slope(h):
    return 2.0 ** (-8.0 * (h + 1) / N_HEADS)


def _ada_kernel(c_ref, w_ref, b_ref, o_ref):
    c = c_ref[...]
    o_ref[...] = jnp.dot(c * jax.nn.sigmoid(c), w_ref[...], precision=lax.Precision.HIGHEST,
                         preferred_element_type=F32) + b_ref[...]


def _ada_mod(c_all, w_ada, b_ada):
    depth, d, d6 = w_ada.shape
    n = c_all.shape[0]
    return pl.pallas_call(
        _ada_kernel,
        grid=(depth, d6 // d),
        in_specs=[pl.BlockSpec((n, d), lambda l, j: (0, 0)),
                  pl.BlockSpec((None, d, d), lambda l, j: (l, 0, j)),
                  pl.BlockSpec((None, 1, d), lambda l, j: (l, 0, j))],
        out_specs=pl.BlockSpec((None, n, d), lambda l, j: (l, 0, j)),
        out_shape=jax.ShapeDtypeStruct((depth, n, d6), F32),
        compiler_params=_cparams(("parallel", "parallel")),
        name="ada_mod",
    )(c_all, w_ada, b_ada.reshape(depth, 1, d6))


def _mod_spec(mod, col, tile, tiles_per_seq):
    d = mod.shape[-1] // 6
    if mod.ndim == 3:
        return pl.BlockSpec((None, 1, d), lambda i, *_: (i // tiles_per_seq, 0, col))
    return pl.BlockSpec((tile, d), lambda i, *_: (i, col))


def _proj_kernel(x_ref, sh_ref, sc_ref, g_ref, w_ref, q0_ref, q1_ref, k_ref, v_ref, gb_ref, u_ref):
    h = _rms(x_ref[...], g_ref[...]) * (1.0 + sc_ref[...]) + sh_ref[...]
    p = jnp.dot(h.astype(BF16), w_ref[...], preferred_element_type=F32)
    q0_ref[...] = p[:, 0:KV_DIM]
    q1_ref[...] = p[:, KV_DIM:2 * KV_DIM]
    k_ref[...] = p[:, 2 * KV_DIM:3 * KV_DIM]
    v_ref[...] = p[:, 3 * KV_DIM:4 * KV_DIM]
    c0 = 4 * KV_DIM
    cd = (p.shape[1] - c0) // 3
    gb_ref[...] = p[:, c0:c0 + cd]
    u_ref[...] = p[:, c0 + cd:c0 + 2 * cd] * p[:, c0 + 2 * cd:c0 + 3 * cd]


def _proj_in(x, mod, g1, w_in, tile, tiles_per_seq):
    t, d = x.shape
    n_in = w_in.shape[1]
    cd = (n_in - 4 * KV_DIM) // 3
    row = lambda i: (i, 0)
    outs = [jax.ShapeDtypeStruct((t, KV_DIM), F32)] * 4 + [jax.ShapeDtypeStruct((t, cd), F32)] * 2
    out_specs = [pl.BlockSpec((tile, KV_DIM), row)] * 4 + [pl.BlockSpec((tile, cd), row)] * 2
    return pl.pallas_call(
        _proj_kernel,
        grid=(t // tile,),
        in_specs=[pl.BlockSpec((tile, d), row),
                  _mod_spec(mod, 0, tile, tiles_per_seq),
                  _mod_spec(mod, 1, tile, tiles_per_seq),
                  pl.BlockSpec((1, d), lambda i: (0, 0)),
                  pl.BlockSpec((d, n_in), lambda i: (0, 0))],
        out_specs=out_specs,
        out_shape=outs,
        compiler_params=_cparams(("parallel",)),
        name="proj_in",
    )(x, mod, mod, g1, w_in)


def _moba_prompt_kernel(q0_ref, q1_ref, k_ref, v_ref, o_ref, kb_s, vb_s, km_s):
    c = pl.program_id(1)
    s_len = k_ref.shape[0]
    nb = s_len // MOBA_BLOCK

    @pl.when(c == 0)
    def _():
        kf = k_ref[...]
        kb_s[...] = kf.astype(BF16)
        vb_s[...] = v_ref[...].astype(BF16)
        km_s[...] = jnp.sum(kf.reshape(nb, MOBA_BLOCK, KV_DIM), axis=1) * (1.0 / MOBA_BLOCK)

    own = (c * Q_CHUNK) // MOBA_BLOCK
    tq = c * Q_CHUNK + lax.broadcasted_iota(jnp.int32, (Q_CHUNK, s_len), 0)
    tk = lax.broadcasted_iota(jnp.int32, (Q_CHUNK, s_len), 1)
    d = (tq - tk).astype(F32)
    dpos = jnp.where(d >= 0.0, d, POS_BIG)
    lane_head = lax.broadcasted_iota(jnp.int32, (Q_CHUNK, KV_DIM), 1) // HEAD_DIM
    blk_of_key = (lax.broadcasted_iota(jnp.int32, (LANES, s_len), 1) // MOBA_BLOCK
                  == lax.broadcasted_iota(jnp.int32, (LANES, s_len), 0)).astype(BF16)
    blk_row = lax.broadcasted_iota(jnp.int32, (nb, Q_CHUNK), 0)

    kb = kb_s[...]
    vb = vb_s[...]
    km = jnp.where(lax.broadcasted_iota(jnp.int32, (nb, KV_DIM), 0) < own, km_s[...], 0.0)
    outs = []
    for h in range(N_HEADS):
        hk, g = divmod(h, Q_GROUP)
        qf = (q0_ref if g == 0 else q1_ref)[...]
        qpad = jnp.where(lane_head == hk, qf, 0.0)
        s = lax.dot_general(qpad.astype(BF16), kb, NT_DIMS, preferred_element_type=F32)
        gate = lax.dot_general(km, qpad, NT_DIMS, precision=lax.Precision.HIGHEST,
                               preferred_element_type=F32)
        gate = jnp.where(blk_row < own, gate, -jnp.inf)
        rank = jnp.zeros((nb, Q_CHUNK), F32)
        for m in range(nb):
            gm = gate[m:m + 1, :]
            beats = (gm > gate) | ((gm == gate) & (blk_row > m))
            rank = rank + jnp.where(beats, 1.0, 0.0)
        allowed = (rank < float(MOBA_TOPK)) | (blk_row >= own)
        neg_t = jnp.where(allowed, 0.0, NEG_BIG)
        neg_t = jnp.concatenate([neg_t, jnp.zeros((LANES - nb, Q_CHUNK), F32)], axis=0)
        bias = jnp.dot(neg_t.T.astype(BF16), blk_of_key, preferred_element_type=F32)
        st = s - _alibi_slope(h) * dpos + bias
        mx = jnp.max(st, axis=1, keepdims=True)
        p = jnp.exp(st - mx)
        l = jnp.sum(p, axis=1, keepdims=True)
        o = jnp.dot(p.astype(BF16), vb, preferred_element_type=F32)
        outs.append(o[:, hk * HEAD_DIM:(hk + 1) * HEAD_DIM] / l)
    o_ref[...] = jnp.concatenate(outs, axis=1)


def _moba_prompt(q0, q1, k, v, batch, seq):
    n_chunks = seq // Q_CHUNK
    nb = seq // MOBA_BLOCK
    qspec = pl.BlockSpec((Q_CHUNK, KV_DIM), lambda b, c: (b * n_chunks + c, 0))
    kvspec = pl.BlockSpec((seq, KV_DIM), lambda b, c: (b, 0))
    return pl.pallas_call(
        _moba_prompt_kernel,
        grid=(batch, n_chunks),
        in_specs=[qspec, qspec, kvspec, kvspec],
        out_specs=pl.BlockSpec((Q_CHUNK, ATTN_DIM), lambda b, c: (b * n_chunks + c, 0)),
        out_shape=jax.ShapeDtypeStruct((batch * seq, ATTN_DIM), F32),
        scratch_shapes=[pltpu.VMEM((seq, KV_DIM), BF16), pltpu.VMEM((seq, KV_DIM), BF16),
                        pltpu.VMEM((nb, KV_DIM), F32)],
        compiler_params=_cparams(("parallel", "arbitrary")),
        name="moba_prompt",
    )(q0, q1, k, v)


def _moba_decode_kernel(pt_ref, qs_ref, kn_ref, vn_ref, *refs, pps, n_blk, pos0, page):
    del pt_ref
    kp = refs[:pps]
    vp = refs[pps:2 * pps]
    o_ref = refs[2 * pps]
    m_s, l_s, g_s, o_s = refs[2 * pps + 1:]
    step = pl.program_id(1)
    n_rows = KV_HEADS * qs_ref.shape[0]
    sd = qs_ref.shape[0] // Q_GROUP
    ppb = MOBA_BLOCK // page
    bps = pps // ppb

    qs = qs_ref[...]
    lane_head = lax.broadcasted_iota(jnp.int32, qs.shape, 1) // HEAD_DIM
    qpad = jnp.concatenate([jnp.where(lane_head == hk, qs, 0.0) for hk in range(KV_HEADS)], axis=0)
    qpad_b = qpad.astype(BF16)
    row = lax.broadcasted_iota(jnp.int32, (n_rows, 1), 0)
    head = (row // qs.shape[0]) * Q_GROUP + (row % qs.shape[0]) // sd
    qi = row % sd
    slope = jnp.zeros((n_rows, 1), F32)
    for h in range(N_HEADS):
        slope = jnp.where(head == h, _alibi_slope(h), slope)
    lane = lax.broadcasted_iota(jnp.int32, (n_rows, LANES), 1)

    @pl.when(step == 0)
    def _():
        m_s[...] = jnp.zeros_like(m_s)
        l_s[...] = jnp.zeros_like(l_s)
        g_s[...] = jnp.zeros_like(g_s)

    for n in range(bps):
        blk = step * bps + n
        kf = jnp.concatenate([kp[n * ppb + i][...] for i in range(ppb)], axis=0)
        vf = jnp.concatenate([vp[n * ppb + i][...] for i in range(ppb)], axis=0)
        s = lax.dot_general(qpad_b, kf.astype(BF16), NT_DIMS, preferred_element_type=F32)
        tk = blk * MOBA_BLOCK + lax.broadcasted_iota(jnp.int32, (1, MOBA_BLOCK), 1)
        dist = ((pos0 + qi) - tk).astype(F32)
        st = s - slope * dist
        mx = jnp.max(st, axis=1, keepdims=True)
        p = jnp.exp(st - mx)
        o_s[blk] = jnp.dot(p.astype(BF16), vf.astype(BF16), preferred_element_type=F32)
        kmean = jnp.sum(kf, axis=0, keepdims=True) * (1.0 / MOBA_BLOCK)
        gate = jnp.sum(qpad * kmean, axis=1, keepdims=True)
        here = lane == blk
        m_s[...] = jnp.where(here, mx, m_s[...])
        l_s[...] = jnp.where(here, jnp.sum(p, axis=1, keepdims=True), l_s[...])
        g_s[...] = jnp.where(here, gate, g_s[...])

    @pl.when(step == pl.num_programs(1) - 1)
    def _():
        valid = lane < n_blk
        work = jnp.where(valid, g_s[...], -jnp.inf)
        sel = jnp.zeros((n_rows, LANES), F32)
        for _ in range(min(MOBA_TOPK, n_blk)):
            top = jnp.max(work, axis=1, keepdims=True)
            first = jnp.min(jnp.where(work == top, lane, LANES), axis=1, keepdims=True)
            hit = (lane == first) & valid
            sel = jnp.where(hit, 1.0, sel)
            work = jnp.where(hit, -jnp.inf, work)
        picked = sel > 0.0
        own_scores = []
        for j in range(sd):
            sj = jnp.sum(qpad * kn_ref[j:j + 1, :], axis=1, keepdims=True)
            dj = (qi - j).astype(F32)
            own_scores.append(jnp.where(dj >= 0.0, sj - slope * dj, NEG_BIG))
        m_tot = jnp.max(jnp.where(picked, m_s[...], NEG_BIG), axis=1, keepdims=True)
        for sj in own_scores:
            m_tot = jnp.maximum(m_tot, sj)
        wn = jnp.where(picked, jnp.exp(m_s[...] - m_tot), 0.0)
        denom = jnp.sum(wn * l_s[...], axis=1, keepdims=True)
        acc = jnp.zeros((n_rows, KV_DIM), F32)
        for j, sj in enumerate(own_scores):
            pj = jnp.exp(sj - m_tot)
            denom = denom + pj
            acc = acc + pj * vn_ref[j:j + 1, :]
        for n in range(n_blk):
            coef = jnp.sum(jnp.where(lane == n, wn, 0.0), axis=1, keepdims=True)
            acc = acc + coef * o_s[n]
        o_ref[...] = acc / denom


def _moba_decode(qs, k_new, v_new, cache_k, cache_v, page_table, layer):
    bd, rows8, _ = qs.shape
    sd = k_new.shape[1]
    n_pages = page_table.shape[1]
    page = cache_k.shape[2]
    n_pool = cache_k.shape[1]
    ppb = MOBA_BLOCK // page
    n_blk = n_pages // ppb
    assert n_pages % ppb == 0 and n_blk <= LANES
    pps = min(16, n_pages)
    assert n_pages % pps == 0 and pps % ppb == 0
    ck = cache_k.reshape(cache_k.shape[0], n_pool, page, KV_DIM)
    cv = cache_v.reshape(cache_v.shape[0], n_pool, page, KV_DIM)
    n_rows = KV_HEADS * rows8

    def page_spec(i):
        return pl.BlockSpec((None, None, page, KV_DIM),
                            lambda b, s, pt: (layer, pt[b, s * pps + i], 0, 0))

    seq_spec = lambda r: pl.BlockSpec((None, r, KV_DIM), lambda b, s, pt: (b, 0, 0))
    grid_spec = pltpu.PrefetchScalarGridSpec(
        num_scalar_prefetch=1,
        grid=(bd, n_pages // pps),
        in_specs=[seq_spec(rows8), seq_spec(sd), seq_spec(sd)]
        + [page_spec(i) for i in range(pps)] * 2,
        out_specs=seq_spec(n_rows),
        scratch_shapes=[pltpu.VMEM((n_rows, LANES), F32)] * 3
        + [pltpu.VMEM((n_blk, n_rows, KV_DIM), F32)],
    )
    kern = functools.partial(_moba_decode_kernel, pps=pps, n_blk=n_blk,
                             pos0=n_pages * page, page=page)
    return pl.pallas_call(
        kern,
        grid_spec=grid_spec,
        out_shape=jax.ShapeDtypeStruct((bd, n_rows, KV_DIM), F32),
        compiler_params=_cparams(("parallel", "arbitrary")),
        name="moba_decode",
    )(page_table, qs, k_new, v_new, *([ck] * pps), *([cv] * pps))


def _mix_kernel(*refs, decode, tiles_per_seq):
    if decode:
        (x_ref, attn_ref, gb_ref, u_ref, u1_ref, u2_ref, ga1_ref, sh2_ref, sc2_ref, cw_ref, ga_ref,
         gc_ref, wo_ref, g2_ref, wq_ref, sk_ref, xo_ref, h2_ref, st_ref) = refs
        u0 = u_ref[...]
        u1 = u1_ref[...]
        u2 = u2_ref[...]
    else:
        (x_ref, attn_ref, gb_ref, u_ref, halo_ref, ga1_ref, sh2_ref, sc2_ref, cw_ref, ga_ref,
         gc_ref, wo_ref, g2_ref, wq_ref, sk_ref, xo_ref, h2_ref, st_ref) = refs
        u0 = u_ref[...]
        first = pl.program_id(0) % tiles_per_seq == 0
        halo = jnp.where(first, 0.0, halo_ref[...])
        hr = halo.shape[0]
        rowi = lax.broadcasted_iota(jnp.int32, u0.shape, 0)
        u1 = jnp.where(rowi < 1, halo[hr - 1:hr, :], pltpu.roll(u0, 1, 0))
        u2 = jnp.where(rowi < 1, halo[hr - 2:hr - 1, :],
                       jnp.where(rowi < 2, halo[hr - 1:hr, :], pltpu.roll(u0, 2, 0)))
    cw = cw_ref[...]
    y = cw[0:1, :] * u2
    y = y + cw[1:2, :] * u1
    y = y + cw[2:3, :] * u0
    yc = gb_ref[...] * y
    cat = jnp.concatenate([_rms(attn_ref[...], ga_ref[...]), _rms(yc, gc_ref[...])], axis=1)
    mixed = jnp.dot(cat.astype(BF16), wo_ref[...], preferred_element_type=F32)
    x_new = x_ref[...] + ga1_ref[...] * mixed
    xo_ref[...] = x_new
    h2 = _rms(x_new, g2_ref[...]) * (1.0 + sc2_ref[...]) + sh2_ref[...]
    h2_ref[...] = h2.astype(BF16)
    q = jnp.dot(h2, wq_ref[...], precision=lax.Precision.HIGHEST, preferred_element_type=F32)
    qt = q.T
    for h in range(PEER_HEADS):
        for p in range(2):
            r0 = (h * 2 + p) * PEER_HALF
            st_ref[h, p] = jnp.dot(sk_ref[p], qt[r0:r0 + PEER_HALF, :],
                                   precision=lax.Precision.HIGHEST, preferred_element_type=F32)


def _mix(x, attn, gb, u, shifted, mod, conv_w, g_attn, g_conv, w_o, g2, w_query, sub_keys,
         tile, tiles_per_seq):
    t, d = x.shape
    cd = u.shape[1]
    decode = shifted is not None
    row = lambda i: (i, 0)
    full2 = lambda a: pl.BlockSpec(a.shape, lambda i: (0,) * a.ndim)
    if decode:
        conv_in = [u, shifted[0], shifted[1]]
        conv_specs = [pl.BlockSpec((tile, cd), row)] * 3
    else:
        halo_rows = 8
        conv_in = [u, u]
        conv_specs = [pl.BlockSpec((tile, cd), row),
                      pl.BlockSpec((halo_rows, cd),
                                   lambda i: (jnp.maximum(i * (tile // halo_rows) - 1, 0), 0))]
    kern = functools.partial(_mix_kernel, decode=decode, tiles_per_seq=tiles_per_seq)
    return pl.pallas_call(
        kern,
        grid=(t // tile,),
        in_specs=[pl.BlockSpec((tile, d), row), pl.BlockSpec((tile, attn.shape[1]), row),
                  pl.BlockSpec((tile, cd), row)] + conv_specs
        + [_mod_spec(mod, 2, tile, tiles_per_seq), _mod_spec(mod, 3, tile, tiles_per_seq),
           _mod_spec(mod, 4, tile, tiles_per_seq),
           full2(conv_w), full2(g_attn), full2(g_conv), full2(w_o), full2(g2), full2(w_query),
           full2(sub_keys)],
        out_specs=[pl.BlockSpec((tile, d), row), pl.BlockSpec((tile, d), row),
                   pl.BlockSpec((PEER_HEADS, 2, PEER_KEYS, tile), lambda i: (0, 0, 0, i))],
        out_shape=[jax.ShapeDtypeStruct((t, d), F32), jax.ShapeDtypeStruct((t, d), BF16),
                   jax.ShapeDtypeStruct((PEER_HEADS, 2, PEER_KEYS, t), F32)],
        compiler_params=_cparams(("parallel",)),
        name="mix_scores",
    )(x, attn, gb, *conv_in, mod, mod, mod, conv_w, g_attn, g_conv, w_o, g2, w_query, sub_keys)


def _top_rows(work, n_take, vals_ref):
    n = work.shape[0]
    idx = lax.broadcasted_iota(jnp.int32, work.shape, 0).astype(F32)
    rank = jnp.full(work.shape, float(n_take), F32)
    for r in range(n_take):
        top = jnp.max(work, axis=0, keepdims=True)
        first = jnp.min(jnp.where(work == top, idx, float(n)), axis=0, keepdims=True)
        hit = idx == first
        rank = jnp.where(hit, float(r), rank)
        work = jnp.where(hit, -jnp.inf, work)
        if vals_ref is not None:
            vals_ref[r:r + 1, :] = top
    return rank


def _route_kernel(s_ref, c1_ref, e1_ref, r2_ref, e2_ref, a_s, b_s):
    def head_body(h, carry):
        s1 = s_ref[h, 0]
        s2 = s_ref[h, 1]
        rank1 = _top_rows(s1, PEER_TOPK, a_s)
        rank2 = _top_rows(s2, PEER_TOPK, b_s)
        a = a_s[...]
        b = b_s[...]
        cand = jnp.concatenate([a[r:r + 1, :] + b for r in range(PEER_TOPK)], axis=0)
        crank = _top_rows(cand, PEER_TOPK, None)
        sel = crank < float(PEER_TOPK)
        top0 = a[0:1, :] + b[0:1, :]
        z = jnp.sum(jnp.where(sel, jnp.exp(cand - top0), 0.0), axis=0, keepdims=True)
        self = jnp.where(sel, 1.0, 0.0)
        c1 = jnp.zeros_like(s1)
        for r in range(PEER_TOPK):
            cnt = jnp.sum(self[r * PEER_TOPK:(r + 1) * PEER_TOPK, :], axis=0, keepdims=True)
            c1 = jnp.where(rank1 == float(r), cnt, c1)
        c1_ref[h] = c1
        e1_ref[h] = jnp.exp(s1 - a[0:1, :]) / z
        r2_ref[h] = rank2
        e2_ref[h] = jnp.exp(s2 - b[0:1, :])
        return carry

    lax.fori_loop(0, PEER_HEADS, head_body, 0)


def _route(scores):
    _, _, nk, t = scores.shape
    tile = LANES
    tab = jax.ShapeDtypeStruct((PEER_HEADS, nk, t), F32)
    tspec = pl.BlockSpec((PEER_HEADS, nk, tile), lambda i: (0, 0, i))
    return pl.pallas_call(
        _route_kernel,
        grid=(t // tile,),
        in_specs=[pl.BlockSpec((PEER_HEADS, 2, nk, tile), lambda i: (0, 0, 0, i))],
        out_specs=[tspec] * 4,
        out_shape=[tab] * 4,
        scratch_shapes=[pltpu.VMEM((PEER_TOPK, tile), F32)] * 2,
        compiler_params=_cparams(("parallel",)),
        name="peer_route",
    )(scores)


def _gelu(x):
    return 0.5 * x * (1.0 + lax.erf(x * (2.0 ** -0.5)))


def _peer_kernel(h2_ref, u_ref, vt_ref, c1_ref, e1_ref, r2_ref, e2_ref, x_ref, ga2_ref, gf_ref,
                 o_ref, acc_s, w_s, *, final):
    j = pl.program_id(1)

    @pl.when(j == 0)
    def _():
        acc_s[...] = jnp.zeros_like(acc_s)

    xb = h2_ref[...]
    n_i1 = c1_ref.shape[1]

    def chunk(ci, carry):
        r0 = pl.multiple_of(ci * PEER_KEYS, PEER_KEYS)
        ht = lax.dot_general(u_ref[pl.ds(r0, PEER_KEYS), :], xb, NT_DIMS,
                             preferred_element_type=F32)
        gate = jnp.zeros_like(ht)
        for h in range(PEER_HEADS):
            c1 = c1_ref[h, pl.ds(ci, 1), :]
            e1 = e1_ref[h, pl.ds(ci, 1), :]
            gate = gate + jnp.where(r2_ref[h] < c1, e2_ref[h], 0.0) * e1
        w_s[pl.ds(r0, PEER_KEYS), :] = (_gelu(ht) * gate).astype(BF16)
        return carry

    lax.fori_loop(0, n_i1, chunk, 0)
    acc_s[...] += jnp.dot(vt_ref[...], w_s[...], preferred_element_type=F32)

    @pl.when(j == pl.num_programs(1) - 1)
    def _():
        out = x_ref[...] + ga2_ref[...] * acc_s[...].T
        if final:
            out = _rms(out, gf_ref[...])
        o_ref[...] = out


def _peer(h2, u_tab, vt_tab, tabs, x_new, mod, g_final, tile, tiles_per_seq, final):
    t, d = x_new.shape
    n_exp = u_tab.shape[0]
    e_tile = min(1024, n_exp)
    c1, e1, r2, e2 = tabs
    nk = c1.shape[1]
    row = lambda i, j: (i, 0)
    i1_spec = pl.BlockSpec((PEER_HEADS, e_tile // nk, tile), lambda i, j: (0, j, i))
    i2_spec = pl.BlockSpec((PEER_HEADS, nk, tile), lambda i, j: (0, 0, i))
    return pl.pallas_call(
        functools.partial(_peer_kernel, final=final),
        grid=(t // tile, n_exp // e_tile),
        in_specs=[pl.BlockSpec((tile, d), row),
                  pl.BlockSpec((e_tile, d), lambda i, j: (j, 0)),
                  pl.BlockSpec((d, e_tile), lambda i, j: (0, j)),
                  i1_spec, i1_spec, i2_spec, i2_spec,
                  pl.BlockSpec((tile, d), row),
                  _mod_spec(mod, 5, tile, tiles_per_seq),
                  pl.BlockSpec((1, d), lambda i, j: (0, 0))],
        out_specs=pl.BlockSpec((tile, d), row),
        out_shape=jax.ShapeDtypeStruct((t, d), F32),
        scratch_shapes=[pltpu.VMEM((d, tile), F32), pltpu.VMEM((e_tile, tile), BF16)],
        compiler_params=_cparams(("parallel", "arbitrary")),
        name="peer_dense",
    )(h2, u_tab, vt_tab, c1, e1, r2, e2, x_new, mod, g_final)


def _token_tile(seq, want):
    tile = min(want, seq)
    assert seq % tile == 0
    return tile


def _layer(x, mod, past, conv_prev, weights, batch, seq, layer, final):
    (g1, w_in, conv_w, g_attn, g_conv, w_o, g2, w_query, sub_keys, u_tab, vt_tab, g_final) = weights
    decode = past is not None
    t, d = x.shape
    if decode:
        tile, tps = t, 1
    else:
        tile = _token_tile(seq, 512)
        tps = seq // tile
    q0, q1, k, v, gb, u = _proj_in(x, mod, g1, w_in, tile, tps)
    cd = u.shape[1]
    if decode:
        cache_k, cache_v, page_table = past
        qs = jnp.concatenate([q0.reshape(batch, seq, KV_DIM), q1.reshape(batch, seq, KV_DIM)], axis=1)
        o = _moba_decode(qs, k.reshape(batch, seq, KV_DIM), v.reshape(batch, seq, KV_DIM),
                         cache_k, cache_v, page_table, layer)
        o = o.reshape(batch, KV_HEADS, Q_GROUP, seq, KV_HEADS, HEAD_DIM)
        o = jnp.stack([o[:, hk, :, :, hk, :] for hk in range(KV_HEADS)], axis=1)
        attn = o.transpose(0, 3, 1, 2, 4).reshape(t, ATTN_DIM)
        xp = jnp.concatenate([conv_prev, u.reshape(batch, seq, cd)], axis=1)
        shifted = (xp[:, 1:1 + seq].reshape(t, cd), xp[:, 0:seq].reshape(t, cd))
        conv_new = xp[:, -(CONV_WIDTH - 1):]
    else:
        attn = _moba_prompt(q0, q1, k, v, batch, seq)
        shifted = None
        conv_new = u.reshape(batch, seq, cd)[:, -(CONV_WIDTH - 1):]
    x_new, h2, scores = _mix(x, attn, gb, u, shifted, mod, conv_w, g_attn, g_conv, w_o, g2,
                             w_query, sub_keys, tile, tps)
    tabs = _route(scores)
    x_out = _peer(h2, u_tab, vt_tab, tabs, x_new, mod, g_final, tile, tps, final)
    return x_out, k, v, conv_new


def kernel(x_prompt, x_sample, cache_k, cache_v, state_conv, page_table, c_prompt, c_sample,
           w_ada, b_ada, g_norm1, g_norm2, w_in, conv_w, g_out_attn, g_out_conv, w_o,
           w_query, sub_keys, u_experts, v_experts, g_final):
    b, s, d = x_prompt.shape
    bd, sd, _ = x_sample.shape
    depth = w_ada.shape[0]
    assert s % MOBA_BLOCK == 0 and (bd * sd) % LANES == 0

    mod = _ada_mod(jnp.concatenate([c_prompt, c_sample], axis=0), w_ada, b_ada)

    scale = HEAD_DIM ** -0.5
    cols0 = jnp.concatenate([jnp.arange(HEAD_DIM) + (hk * Q_GROUP) * HEAD_DIM for hk in range(KV_HEADS)])
    cols1 = cols0 + HEAD_DIM

    xp = x_prompt.reshape(b * s, d)
    xs = x_sample.reshape(bd * sd, d)
    zero_conv = jnp.zeros((b, CONV_WIDTH - 1, conv_w.shape[-1]), x_prompt.dtype)
    ks_p, vs_p, cs_p, ks_s, vs_s, cs_s = [], [], [], [], [], []
    for l in range(depth):
        wl = w_in[l]
        w_in_l = jnp.concatenate([wl[:, cols0] * scale, wl[:, cols1] * scale, wl[:, ATTN_DIM:]],
                                 axis=1).astype(BF16)
        weights = (g_norm1[l][None], w_in_l, conv_w[l], g_out_attn[l][None], g_out_conv[l][None],
                   w_o[l].astype(BF16), g_norm2[l][None], w_query[l], sub_keys[l],
                   u_experts[l].astype(BF16), v_experts[l].T.astype(BF16), g_final[None])
        final = l == depth - 1
        mod_p = mod[l, :b][:, None, :]
        mod_s = jnp.repeat(mod[l, b:], sd, axis=0)
        xp, k, v, cn = _layer(xp, mod_p, None, zero_conv, weights, b, s, l, final)
        ks_p.append(k.reshape(b, s, KV_HEADS, HEAD_DIM))
        vs_p.append(v.reshape(b, s, KV_HEADS, HEAD_DIM))
        cs_p.append(cn)
        xs, k, v, cn = _layer(xs, mod_s, (cache_k, cache_v, page_table), state_conv[l], weights,
                              bd, sd, l, final)
        ks_s.append(k.reshape(bd, sd, KV_HEADS, HEAD_DIM))
        vs_s.append(v.reshape(bd, sd, KV_HEADS, HEAD_DIM))
        cs_s.append(cn)
    return (xp.reshape(b, s, d), xs.reshape(bd, sd, d),
            jnp.stack(ks_p), jnp.stack(vs_p), jnp.stack(cs_p),
            jnp.stack(ks_s), jnp.stack(vs_s), jnp.stack(cs_s))
```

```python
import functools

import jax
import jax.numpy as jnp
from jax import lax
from jax.experimental import pallas as pl
from jax.experimental.pallas import tpu as pltpu

F32 = jnp.float32
BF16 = jnp.bfloat16

N_HEADS = 8
KV_HEADS = 4
HEAD_DIM = 64
Q_GROUP = N_HEADS // KV_HEADS
ATTN_DIM = N_HEADS * HEAD_DIM
KV_DIM = KV_HEADS * HEAD_DIM
MOBA_BLOCK = 256
MOBA_TOPK = 3
Q_CHUNK = 128
CONV_WIDTH = 3
PEER_HEADS = 8
PEER_KEYS = 128
PEER_HALF = 64
PEER_TOPK = 16
NORM_EPS = 1e-6
NEG_BIG = -1e30
POS_BIG = 1e30

VMEM_LIMIT_BYTES = 56 * 1024 * 1024
LANES = 128
BF16_ROWS = 16

NT_DIMS = (((1,), (1,)), ((), ()))


def _cparams(sem):
    return pltpu.CompilerParams(dimension_semantics=sem, vmem_limit_bytes=VMEM_LIMIT_BYTES)


def _rms(x, g):
    y = x * lax.rsqrt(jnp.mean(x * x, axis=-1, keepdims=True) + NORM_EPS)
    return y * g


def _alibi_slope(h):
    return 2.0 ** (-8.0 * (h + 1) / N_HEADS)


def _ada_kernel(c_ref, w_ref, b_ref, o_ref):
    c = c_ref[...]
    o_ref[...] = jnp.dot(c * jax.nn.sigmoid(c), w_ref[...], precision=lax.Precision.HIGHEST,
                         preferred_element_type=F32) + b_ref[...]


def _ada_mod(c_all, w_ada, b_ada):
    depth, d, d6 = w_ada.shape
    n = c_all.shape[0]
    return pl.pallas_call(
        _ada_kernel,
        grid=(depth, d6 // d),
        in_specs=[pl.BlockSpec((n, d), lambda l, j: (0, 0)),
                  pl.BlockSpec((None, d, d), lambda l, j: (l, 0, j)),
                  pl.BlockSpec((None, 1, d), lambda l, j: (l, 0, j))],
        out_specs=pl.BlockSpec((None, n, d), lambda l, j: (l, 0, j)),
        out_shape=jax.ShapeDtypeStruct((depth, n, d6), F32),
        compiler_params=_cparams(("parallel", "parallel")),
        name="ada_mod",
    )(c_all, w_ada, b_ada.reshape(depth, 1, d6))


def _mod_spec(mod, col, tile, tiles_per_seq):
    d = mod.shape[-1] // 6
    if mod.ndim == 3:
        return pl.BlockSpec((None, 1, d), lambda i, *_: (i // tiles_per_seq, 0, col))
    return pl.BlockSpec((tile, d), lambda i, *_: (i, col))


def _proj_kernel(x_ref, sh_ref, sc_ref, g_ref, w_ref, q0_ref, q1_ref, k_ref, v_ref, gb_ref, u_ref):
    h = _rms(x_ref[...], g_ref[...]) * (1.0 + sc_ref[...]) + sh_ref[...]
    p = jnp.dot(h.astype(BF16), w_ref[...], preferred_element_type=F32)
    q0_ref[...] = p[:, 0:KV_DIM]
    q1_ref[...] = p[:, KV_DIM:2 * KV_DIM]
    k_ref[...] = p[:, 2 * KV_DIM:3 * KV_DIM]
    v_ref[...] = p[:, 3 * KV_DIM:4 * KV_DIM]
    c0 = 4 * KV_DIM
    cd = (p.shape[1] - c0) // 3
    gb_ref[...] = p[:, c0:c0 + cd]
    u_ref[...] = p[:, c0 + cd:c0 + 2 * cd] * p[:, c0 + 2 * cd:c0 + 3 * cd]


def _proj_in(x, mod, g1, w_in, tile, tiles_per_seq):
    t, d = x.shape
    n_in = w_in.shape[1]
    cd = (n_in - 4 * KV_DIM) // 3
    row = lambda i: (i, 0)
    outs = [jax.ShapeDtypeStruct((t, KV_DIM), F32)] * 4 + [jax.ShapeDtypeStruct((t, cd), F32)] * 2
    out_specs = [pl.BlockSpec((tile, KV_DIM), row)] * 4 + [pl.BlockSpec((tile, cd), row)] * 2
    return pl.pallas_call(
        _proj_kernel,
        grid=(t // tile,),
        in_specs=[pl.BlockSpec((tile, d), row),
                  _mod_spec(mod, 0, tile, tiles_per_seq),
                  _mod_spec(mod, 1, tile, tiles_per_seq),
                  pl.BlockSpec((1, d), lambda i: (0, 0)),
                  pl.BlockSpec((d, n_in), lambda i: (0, 0))],
        out_specs=out_specs,
        out_shape=outs,
        compiler_params=_cparams(("parallel",)),
        name="proj_in",
    )(x, mod, mod, g1, w_in)


def _moba_prompt_kernel(q0_ref, q1_ref, k_ref, v_ref, o_ref, kb_s, vb_s, km4_s, e_s):
    c = pl.program_id(1)
    s_len = k_ref.shape[0]
    nb = s_len // MOBA_BLOCK

    @pl.when(c == 0)
    def _():
        kf = k_ref[...]
        kb_s[...] = kf.astype(BF16)
        vb_s[...] = v_ref[...].astype(BF16)
        km = jnp.sum(kf.reshape(nb, MOBA_BLOCK, KV_DIM), axis=1) * (1.0 / MOBA_BLOCK)
        lane_head = lax.broadcasted_iota(jnp.int32, km.shape, 1) // HEAD_DIM
        parts = [jnp.where(lane_head == hk, km, 0.0) for hk in range(KV_HEADS)]
        parts.append(jnp.zeros((LANES - KV_HEADS * nb, KV_DIM), F32))
        km4_s[...] = jnp.concatenate(parts, axis=0).T
        key_blk = lax.broadcasted_iota(jnp.int32, (LANES, s_len), 1) // MOBA_BLOCK
        row = lax.broadcasted_iota(jnp.int32, (LANES, s_len), 0)
        for hk in range(KV_HEADS):
            e_s[hk] = (key_blk + hk * nb == row).astype(BF16)

    chunks_per_blk = MOBA_BLOCK // Q_CHUNK
    own = c // chunks_per_blk
    for n_past in range(nb):
        @pl.when(own == n_past)
        def _(n_past=n_past):
            _moba_prompt_chunk(n_past, (c % chunks_per_blk) * Q_CHUNK, q0_ref, q1_ref, o_ref,
                               kb_s, vb_s, km4_s, e_s)


def _block_penalty(q, km4, n_past, nb):
    gate = jnp.dot(q, km4, precision=lax.Precision.HIGHEST, preferred_element_type=F32)
    n_of = lax.broadcasted_iota(jnp.int32, gate.shape, 1) & (nb - 1)
    gate = jnp.where(n_of < n_past, gate, -jnp.inf)
    rank = jnp.zeros(gate.shape, F32)
    for delta in range(1, n_past):
        above = pltpu.roll(gate, LANES - delta, 1)
        rank = rank + jnp.where(above > gate, jnp.where(n_of + delta < nb, 1.0, 0.0), 0.0)
        below = pltpu.roll(gate, delta, 1)
        rank = rank + jnp.where(below >= gate, jnp.where(n_of >= delta, 1.0, 0.0), 0.0)
    return jnp.where(rank < float(MOBA_TOPK), 0.0, NEG_BIG).astype(BF16)


def _moba_prompt_chunk(n_past, q_off, q0_ref, q1_ref, o_ref, kb_s, vb_s, km4_s, e_s):
    nb = kb_s.shape[0] // MOBA_BLOCK
    n_keys = (n_past + 1) * MOBA_BLOCK
    past_keys = n_past * MOBA_BLOCK
    tk = lax.broadcasted_iota(jnp.int32, (1, n_keys), 1).astype(F32)
    q_pos = q_off + lax.broadcasted_iota(jnp.int32, (Q_CHUNK, MOBA_BLOCK), 0)
    causal = jnp.where(q_pos >= lax.broadcasted_iota(jnp.int32, (Q_CHUNK, MOBA_BLOCK), 1), 0.0, NEG_BIG)
    lane_head = lax.broadcasted_iota(jnp.int32, (Q_CHUNK, KV_DIM), 1) // HEAD_DIM
    need_gate = n_past > MOBA_TOPK
    qs = (q0_ref[...], q1_ref[...])
    if need_gate:
        km4 = km4_s[...]
        penalty = [_block_penalty(q, km4, n_past, nb) for q in qs]
    kb = kb_s[0:n_keys, :]
    vb = vb_s[0:n_keys, :]
    outs = []
    for h in range(N_HEADS):
        hk, g = divmod(h, Q_GROUP)
        qpad = jnp.where(lane_head == hk, qs[g], 0.0)
        s = lax.dot_general(qpad.astype(BF16), kb, NT_DIMS, preferred_element_type=F32)
        st = s + _alibi_slope(h) * tk
        if need_gate:
            bias = jnp.dot(penalty[g], e_s[hk, :, 0:past_keys], preferred_element_type=F32)
            st = jnp.concatenate([st[:, :past_keys] + bias, st[:, past_keys:] + causal], axis=1)
        elif n_past > 0:
            st = jnp.concatenate([st[:, :past_keys], st[:, past_keys:] + causal], axis=1)
        else:
            st = st + causal
        mx = jnp.max(st, axis=1, keepdims=True)
        p = jnp.exp(st - mx)
        l = jnp.sum(p, axis=1, keepdims=True)
        o = jnp.dot(p.astype(BF16), vb, preferred_element_type=F32)
        outs.append(o[:, hk * HEAD_DIM:(hk + 1) * HEAD_DIM] / l)
    o_ref[...] = jnp.concatenate(outs, axis=1)


def _moba_prompt(q0, q1, k, v, batch, seq):
    n_chunks = seq // Q_CHUNK
    nb = seq // MOBA_BLOCK
    assert nb & (nb - 1) == 0 and KV_HEADS * nb <= LANES
    qspec = pl.BlockSpec((Q_CHUNK, KV_DIM), lambda b, c: (b * n_chunks + c, 0))
    kvspec = pl.BlockSpec((seq, KV_DIM), lambda b, c: (b, 0))
    return pl.pallas_call(
        _moba_prompt_kernel,
        grid=(batch, n_chunks),
        in_specs=[qspec, qspec, kvspec, kvspec],
        out_specs=pl.BlockSpec((Q_CHUNK, ATTN_DIM), lambda b, c: (b * n_chunks + c, 0)),
        out_shape=jax.ShapeDtypeStruct((batch * seq, ATTN_DIM), F32),
        scratch_shapes=[pltpu.VMEM((seq, KV_DIM), BF16), pltpu.VMEM((seq, KV_DIM), BF16),
                        pltpu.VMEM((KV_DIM, LANES), F32), pltpu.VMEM((KV_HEADS, LANES, seq), BF16)],
        compiler_params=_cparams(("parallel", "arbitrary")),
        name="moba_prompt",
    )(q0, q1, k, v)


def _moba_decode_kernel(pt_ref, qs_ref, kn_ref, vn_ref, *refs, pps, n_blk, pos0, page):
    del pt_ref
    kp = refs[:pps]
    vp = refs[pps:2 * pps]
    o_ref = refs[2 * pps]
    m_s, l_s, g_s, o_s = refs[2 * pps + 1:]
    step = pl.program_id(1)
    n_rows = KV_HEADS * qs_ref.shape[0]
    sd = qs_ref.shape[0] // Q_GROUP
    ppb = MOBA_BLOCK // page
    bps = pps // ppb

    qs = qs_ref[...]
    lane_head = lax.broadcasted_iota(jnp.int32, qs.shape, 1) // HEAD_DIM
    qpad = jnp.concatenate([jnp.where(lane_head == hk, qs, 0.0) for hk in range(KV_HEADS)], axis=0)
    qpad_b = qpad.astype(BF16)
    row = lax.broadcasted_iota(jnp.int32, (n_rows, 1), 0)
    head = (row // qs.shape[0]) * Q_GROUP + (row % qs.shape[0]) // sd
    qi = row % sd
    slope = jnp.zeros((n_rows, 1), F32)
    for h in range(N_HEADS):
        slope = jnp.where(head == h, _alibi_slope(h), slope)
    lane = lax.broadcasted_iota(jnp.int32, (n_rows, LANES), 1)

    @pl.when(step == 0)
    def _():
        m_s[...] = jnp.zeros_like(m_s)
        l_s[...] = jnp.zeros_like(l_s)
        g_s[...] = jnp.zeros_like(g_s)

    for n in range(bps):
        blk = step * bps + n
        kf = jnp.concatenate([kp[n * ppb + i][...] for i in range(ppb)], axis=0)
        vf = jnp.concatenate([vp[n * ppb + i][...] for i in range(ppb)], axis=0)
        s = lax.dot_general(qpad_b, kf.astype(BF16), NT_DIMS, preferred_element_type=F32)
        tk = blk * MOBA_BLOCK + lax.broadcasted_iota(jnp.int32, (1, MOBA_BLOCK), 1)
        dist = ((pos0 + qi) - tk).astype(F32)
        st = s - slope * dist
        mx = jnp.max(st, axis=1, keepdims=True)
        p = jnp.exp(st - mx)
        o_s[blk] = jnp.dot(p.astype(BF16), vf.astype(BF16), preferred_element_type=F32)
        kmean = jnp.sum(kf, axis=0, keepdims=True) * (1.0 / MOBA_BLOCK)
        gate = jnp.sum(qpad * kmean, axis=1, keepdims=True)
        here = lane == blk
        m_s[...] = jnp.where(here, mx, m_s[...])
        l_s[...] = jnp.where(here, jnp.sum(p, axis=1, keepdims=True), l_s[...])
        g_s[...] = jnp.where(here, gate, g_s[...])

    @pl.when(step == pl.num_programs(1) - 1)
    def _():
        valid = lane < n_blk
        work = jnp.where(valid, g_s[...], -jnp.inf)
        sel = jnp.zeros((n_rows, LANES), F32)
        for _ in range(min(MOBA_TOPK, n_blk)):
            top = jnp.max(work, axis=1, keepdims=True)
            first = jnp.min(jnp.where(work == top, lane, LANES), axis=1, keepdims=True)
            hit = (lane == first) & valid
            sel = jnp.where(hit, 1.0, sel)
            work = jnp.where(hit, -jnp.inf, work)
        picked = sel > 0.0
        own_scores = []
        for j in range(sd):
            sj = jnp.sum(qpad * kn_ref[j:j + 1, :], axis=1, keepdims=True)
            dj = (qi - j).astype(F32)
            own_scores.append(jnp.where(dj >= 0.0, sj - slope * dj, NEG_BIG))
        m_tot = jnp.max(jnp.where(picked, m_s[...], NEG_BIG), axis=1, keepdims=True)
        for sj in own_scores:
            m_tot = jnp.maximum(m_tot, sj)
        wn = jnp.where(picked, jnp.exp(m_s[...] - m_tot), 0.0)
        denom = jnp.sum(wn * l_s[...], axis=1, keepdims=True)
        acc = jnp.zeros((n_rows, KV_DIM), F32)
        for j, sj in enumerate(own_scores):
            pj = jnp.exp(sj - m_tot)
            denom = denom + pj
            acc = acc + pj * vn_ref[j:j + 1, :]
        for n in range(n_blk):
            coef = jnp.sum(jnp.where(lane == n, wn, 0.0), axis=1, keepdims=True)
            acc = acc + coef * o_s[n]
        o_ref[...] = acc / denom


def _moba_decode(qs, k_new, v_new, cache_k, cache_v, page_table, layer):
    bd, rows8, _ = qs.shape
    sd = k_new.shape[1]
    n_pages = page_table.shape[1]
    page = cache_k.shape[2]
    n_pool = cache_k.shape[1]
    ppb = MOBA_BLOCK // page
    n_blk = n_pages // ppb
    assert n_pages % ppb == 0 and n_blk <= LANES
    pps = min(16, n_pages)
    assert n_pages % pps == 0 and pps % ppb == 0
    ck = cache_k.reshape(cache_k.shape[0], n_pool, page, KV_DIM)
    cv = cache_v.reshape(cache_v.shape[0], n_pool, page, KV_DIM)
    n_rows = KV_HEADS * rows8

    def page_spec(i):
        return pl.BlockSpec((None, None, page, KV_DIM),
                            lambda b, s, pt: (layer, pt[b, s * pps + i], 0, 0))

    seq_spec = lambda r: pl.BlockSpec((None, r, KV_DIM), lambda b, s, pt: (b, 0, 0))
    grid_spec = pltpu.PrefetchScalarGridSpec(
        num_scalar_prefetch=1,
        grid=(bd, n_pages // pps),
        in_specs=[seq_spec(rows8), seq_spec(sd), seq_spec(sd)]
        + [page_spec(i) for i in range(pps)] * 2,
        out_specs=seq_spec(n_rows),
        scratch_shapes=[pltpu.VMEM((n_rows, LANES), F32)] * 3
        + [pltpu.VMEM((n_blk, n_rows, KV_DIM), F32)],
    )
    kern = functools.partial(_moba_decode_kernel, pps=pps, n_blk=n_blk,
                             pos0=n_pages * page, page=page)
    return pl.pallas_call(
        kern,
        grid_spec=grid_spec,
        out_shape=jax.ShapeDtypeStruct((bd, n_rows, KV_DIM), F32),
        compiler_params=_cparams(("parallel", "arbitrary")),
        name="moba_decode",
    )(page_table, qs, k_new, v_new, *([ck] * pps), *([cv] * pps))


def _mix_kernel(*refs, decode, tiles_per_seq):
    if decode:
        (x_ref, attn_ref, gb_ref, u_ref, u1_ref, u2_ref, ga1_ref, sh2_ref, sc2_ref, cw_ref, ga_ref,
         gc_ref, wo_ref, g2_ref, wq_ref, sk_ref, xo_ref, h2_ref, st_ref) = refs
        u0 = u_ref[...]
        u1 = u1_ref[...]
        u2 = u2_ref[...]
    else:
        (x_ref, attn_ref, gb_ref, u_ref, halo_ref, ga1_ref, sh2_ref, sc2_ref, cw_ref, ga_ref,
         gc_ref, wo_ref, g2_ref, wq_ref, sk_ref, xo_ref, h2_ref, st_ref) = refs
        u0 = u_ref[...]
        first = pl.program_id(0) % tiles_per_seq == 0
        halo = jnp.where(first, 0.0, halo_ref[...])
        hr = halo.shape[0]
        rowi = lax.broadcasted_iota(jnp.int32, u0.shape, 0)
        u1 = jnp.where(rowi < 1, halo[hr - 1:hr, :], pltpu.roll(u0, 1, 0))
        u2 = jnp.where(rowi < 1, halo[hr - 2:hr - 1, :],
                       jnp.where(rowi < 2, halo[hr - 1:hr, :], pltpu.roll(u0, 2, 0)))
    cw = cw_ref[...]
    y = cw[0:1, :] * u2
    y = y + cw[1:2, :] * u1
    y = y + cw[2:3, :] * u0
    yc = gb_ref[...] * y
    cat = jnp.concatenate([_rms(attn_ref[...], ga_ref[...]), _rms(yc, gc_ref[...])], axis=1)
    mixed = jnp.dot(cat.astype(BF16), wo_ref[...], preferred_element_type=F32)
    x_new = x_ref[...] + ga1_ref[...] * mixed
    xo_ref[...] = x_new
    h2 = _rms(x_new, g2_ref[...]) * (1.0 + sc2_ref[...]) + sh2_ref[...]
    h2_ref[...] = h2.astype(BF16)
    q = jnp.dot(h2, wq_ref[...], precision=lax.Precision.HIGHEST, preferred_element_type=F32)
    qt = q.T
    for h in range(PEER_HEADS):
        for p in range(2):
            r0 = (h * 2 + p) * PEER_HALF
            st_ref[h, p] = jnp.dot(sk_ref[p], qt[r0:r0 + PEER_HALF, :],
                                   precision=lax.Precision.HIGHEST, preferred_element_type=F32)


def _mix(x, attn, gb, u, shifted, mod, conv_w, g_attn, g_conv, w_o, g2, w_query, sub_keys,
         tile, tiles_per_seq):
    t, d = x.shape
    cd = u.shape[1]
    decode = shifted is not None
    row = lambda i: (i, 0)
    full2 = lambda a: pl.BlockSpec(a.shape, lambda i: (0,) * a.ndim)
    if decode:
        conv_in = [u, shifted[0], shifted[1]]
        conv_specs = [pl.BlockSpec((tile, cd), row)] * 3
    else:
        halo_rows = 8
        conv_in = [u, u]
        conv_specs = [pl.BlockSpec((tile, cd), row),
                      pl.BlockSpec((halo_rows, cd),
                                   lambda i: (jnp.maximum(i * (tile // halo_rows) - 1, 0), 0))]
    kern = functools.partial(_mix_kernel, decode=decode, tiles_per_seq=tiles_per_seq)
    return pl.pallas_call(
        kern,
        grid=(t // tile,),
        in_specs=[pl.BlockSpec((tile, d), row), pl.BlockSpec((tile, attn.shape[1]), row),
                  pl.BlockSpec((tile, cd), row)] + conv_specs
        + [_mod_spec(mod, 2, tile, tiles_per_seq), _mod_spec(mod, 3, tile, tiles_per_seq),
           _mod_spec(mod, 4, tile, tiles_per_seq),
           full2(conv_w), full2(g_attn), full2(g_conv), full2(w_o), full2(g2), full2(w_query),
           full2(sub_keys)],
        out_specs=[pl.BlockSpec((tile, d), row), pl.BlockSpec((tile, d), row),
                   pl.BlockSpec((PEER_HEADS, 2, PEER_KEYS, tile), lambda i: (0, 0, 0, i))],
        out_shape=[jax.ShapeDtypeStruct((t, d), F32), jax.ShapeDtypeStruct((t, d), BF16),
                   jax.ShapeDtypeStruct((PEER_HEADS, 2, PEER_KEYS, t), F32)],
        compiler_params=_cparams(("parallel",)),
        name="mix_scores",
    )(x, attn, gb, *conv_in, mod, mod, mod, conv_w, g_attn, g_conv, w_o, g2, w_query, sub_keys)


def _top_rows(work, n_take, vals_ref):
    n = work.shape[0]
    idx = lax.broadcasted_iota(jnp.int32, work.shape, 0).astype(F32)
    rank = jnp.full(work.shape, float(n_take), F32)
    for r in range(n_take):
        top = jnp.max(work, axis=0, keepdims=True)
        first = jnp.min(jnp.where(work == top, idx, float(n)), axis=0, keepdims=True)
        hit = idx == first
        rank = jnp.where(hit, float(r), rank)
        work = jnp.where(hit, -jnp.inf, work)
        if vals_ref is not None:
            vals_ref[r:r + 1, :] = top
    return rank


def _top_rows_distinct(work, n_take, vals_ref):
    rank = jnp.full(work.shape, float(n_take), F32)
    for r in range(n_take):
        top = jnp.max(work, axis=0, keepdims=True)
        hit = work == top
        rank = jnp.where(hit, float(r), rank)
        work = jnp.where(hit, -jnp.inf, work)
        if vals_ref is not None:
            vals_ref[r:r + 1, :] = top
    n_marked = jnp.sum(jnp.where(rank < float(n_take), 1.0, 0.0), axis=0, keepdims=True)
    return rank, n_marked


def _dup_bf16(x):
    u = lax.bitcast_convert_type(x, jnp.uint32)
    r = (u + jnp.uint32(0x7FFF) + ((u >> 16) & jnp.uint32(1))) >> 16
    return r | (r << 16)


def _write_tables(h, refs, s1, s2, rank1, rank2, c1, z, a0, b0):
    c1_ref, e1_ref, r2_ref, e2_ref = refs
    c1_ref[h] = _dup_bf16(c1)
    e1_ref[h] = _dup_bf16(jnp.exp(s1 - a0) / z)
    r2_ref[h] = pltpu.bitcast(rank2.astype(BF16), jnp.uint32)
    e2_ref[h] = pltpu.bitcast(jnp.exp(s2 - b0).astype(BF16), jnp.uint32)


def _route_head_exact(h, s_ref, refs, a_s, b_s):
    s1 = s_ref[h, 0]
    s2 = s_ref[h, 1]
    rank1 = _top_rows(s1, PEER_TOPK, a_s)
    rank2 = _top_rows(s2, PEER_TOPK, b_s)
    a = a_s[...]
    b = b_s[...]
    cand = jnp.concatenate([a[r:r + 1, :] + b for r in range(PEER_TOPK)], axis=0)
    crank = _top_rows(cand, PEER_TOPK, None)
    sel = crank < float(PEER_TOPK)
    top0 = a[0:1, :] + b[0:1, :]
    z = jnp.sum(jnp.where(sel, jnp.exp(cand - top0), 0.0), axis=0, keepdims=True)
    self = jnp.where(sel, 1.0, 0.0)
    c1 = jnp.zeros_like(s1)
    for r in range(PEER_TOPK):
        cnt = jnp.sum(self[r * PEER_TOPK:(r + 1) * PEER_TOPK, :], axis=0, keepdims=True)
        c1 = jnp.where(rank1 == float(r), cnt, c1)
    _write_tables(h, refs, s1, s2, rank1, rank2, c1, z, a[0:1, :], b[0:1, :])


def _route_head_distinct(h, s_ref, refs, a_s, b_s):
    s1 = s_ref[h, 0]
    s2 = s_ref[h, 1]
    rank1, n1 = _top_rows_distinct(s1, PEER_TOPK, a_s)
    rank2, n2 = _top_rows_distinct(s2, PEER_TOPK, b_s)
    a = a_s[...]
    b = b_s[...]
    half = PEER_TOPK // 2
    cand = jnp.concatenate([a[0:1, :] + b] + [a[r:r + 1, :] + b[0:half, :] for r in range(1, half)]
                           + [a[half:, :] + b[0:1, :]], axis=0)
    row = lax.broadcasted_iota(jnp.int32, cand.shape, 0)
    assert half == 8
    r_mid = ((row - PEER_TOPK) >> 3) + 1
    c_mid = (row - PEER_TOPK) & 7
    in_mid = (row >= PEER_TOPK) & (row < PEER_TOPK + half * (half - 1))
    cand = jnp.where(in_mid & ((r_mid + 1) * (c_mid + 1) > PEER_TOPK), -jnp.inf, cand)
    crank, nc = _top_rows_distinct(cand, PEER_TOPK, None)
    sel = crank < float(PEER_TOPK)
    top0 = a[0:1, :] + b[0:1, :]
    z = jnp.sum(jnp.where(sel, jnp.exp(cand - top0), 0.0), axis=0, keepdims=True)
    self = jnp.where(sel, 1.0, 0.0)
    cnts = [jnp.sum(self[0:PEER_TOPK, :], axis=0, keepdims=True)]
    for r in range(1, half):
        lo = PEER_TOPK + (r - 1) * half
        cnts.append(jnp.sum(self[lo:lo + half, :], axis=0, keepdims=True))
    tail = PEER_TOPK + half * (half - 1)
    cnts += [self[tail + r:tail + r + 1, :] for r in range(half)]
    c1 = jnp.zeros_like(s1)
    for r in range(PEER_TOPK):
        c1 = jnp.where(rank1 == float(r), cnts[r], c1)
    _write_tables(h, refs, s1, s2, rank1, rank2, c1, z, a[0:1, :], b[0:1, :])
    want = float(PEER_TOPK)
    return jnp.abs(n1 - want) + jnp.abs(n2 - want) + jnp.abs(nc - want)


def _route_kernel(s_ref, c1_ref, e1_ref, r2_ref, e2_ref, a_s, b_s):
    refs = (c1_ref, e1_ref, r2_ref, e2_ref)

    def head_pair(i, carry):
        heads = (2 * i, 2 * i + 1)
        ties = [_route_head_distinct(h, s_ref, refs, a_s.at[k], b_s.at[k]) for k, h in enumerate(heads)]
        for k, h in enumerate(heads):
            @pl.when(jnp.max(ties[k]) > 0.0)
            def _(k=k, h=h):
                _route_head_exact(h, s_ref, refs, a_s.at[k], b_s.at[k])

        return carry

    lax.fori_loop(0, PEER_HEADS // 2, head_pair, 0)


def _route(scores):
    _, _, nk, t = scores.shape
    tile = LANES
    wspec = pl.BlockSpec((PEER_HEADS, nk, tile), lambda i: (0, 0, i))
    pspec = pl.BlockSpec((PEER_HEADS, nk // 2, tile), lambda i: (0, 0, i))
    words = jax.ShapeDtypeStruct((PEER_HEADS, nk, t), jnp.uint32)
    packed = jax.ShapeDtypeStruct((PEER_HEADS, nk // 2, t), jnp.uint32)
    return pl.pallas_call(
        _route_kernel,
        grid=(t // tile,),
        in_specs=[pl.BlockSpec((PEER_HEADS, 2, nk, tile), lambda i: (0, 0, 0, i))],
        out_specs=[wspec, wspec, pspec, pspec],
        out_shape=[words, words, packed, packed],
        scratch_shapes=[pltpu.VMEM((2, PEER_TOPK, tile), F32)] * 2,
        compiler_params=_cparams(("parallel",)),
        name="peer_route",
    )(scores)


def _gelu(x):
    return 0.5 * x * (1.0 + lax.erf(x * (2.0 ** -0.5)))


def _peer_kernel(h2_ref, u_ref, vt_ref, c1_ref, e1_ref, r2_ref, e2_ref, x_ref, ga2_ref, gf_ref,
                 o_ref, acc_s, w_s, gate_a, gate_b, *, final):
    j = pl.program_id(1)
    n_e = pl.num_programs(1) - 1
    slot = j % 2

    @pl.when(j == 0)
    def _():
        acc_s[...] = jnp.zeros_like(acc_s)
        w_s[1] = jnp.zeros(w_s.shape[1:], BF16)

    xb = h2_ref[...]
    tt = xb.shape[0]
    w_prev = w_s.at[1 - slot]
    w_cur = w_s.at[slot]
    grp = PEER_KEYS // BF16_ROWS
    n_chunks = u_ref.shape[0] // PEER_KEYS

    def gate_into(ci, gate_ref):
        c1_rows = [pltpu.bitcast(jnp.broadcast_to(c1_ref[h, pl.ds(ci, 1), :], (8, tt)), BF16)
                   for h in range(PEER_HEADS)]
        e1_rows = [pltpu.bitcast(jnp.broadcast_to(e1_ref[h, pl.ds(ci, 1), :], (8, tt)), BF16)
                   for h in range(PEER_HEADS)]
        for cb in range(tt // LANES):
            cols = slice(cb * LANES, (cb + 1) * LANES)
            gate = jnp.zeros((PEER_KEYS, LANES), BF16)
            for h in range(PEER_HEADS):
                c1 = jnp.concatenate([c1_rows[h][:, cols]] * grp, axis=0)
                e1 = jnp.concatenate([e1_rows[h][:, cols]] * grp, axis=0)
                r2 = pltpu.bitcast(r2_ref[h, :, cols], BF16)
                e2 = pltpu.bitcast(e2_ref[h, :, cols], BF16)
                picked = jnp.where(r2 < c1, e2, jnp.zeros((), BF16))
                gate = gate + picked * e1
            gate_ref[cb] = gate

    def chunk(ci, gate_in, gate_out):
        rows = pl.ds(pl.multiple_of(ci * PEER_KEYS, PEER_KEYS), PEER_KEYS)
        ht = lax.dot_general(u_ref[rows, :], xb, NT_DIMS, preferred_element_type=F32)
        acc_s[rows, :] += jnp.dot(vt_ref[rows, :], w_prev[...], preferred_element_type=F32)
        gate_into(jnp.minimum(ci + 1, n_chunks - 1), gate_out)
        for cb in range(tt // LANES):
            cols = slice(cb * LANES, (cb + 1) * LANES)
            w_cur[rows, cols] = _gelu(ht[:, cols]).astype(BF16) * gate_in[cb]

    def chunk_pair(p, carry):
        chunk(2 * p, gate_a, gate_b)
        chunk(2 * p + 1, gate_b, gate_a)
        return carry

    gate_into(0, gate_a)
    lax.fori_loop(0, n_chunks // 2, chunk_pair, 0)

    @pl.when(j == n_e)
    def _():
        out = x_ref[...] + ga2_ref[...] * acc_s[...].T
        if final:
            out = _rms(out, gf_ref[...])
        o_ref[...] = out


def _peer(h2, u_tab, vt_tab, tabs, x_new, mod, g_final, tile, tiles_per_seq, final):
    t, d = x_new.shape
    n_exp = u_tab.shape[0]
    e_tile = d
    assert n_exp % e_tile == 0 and e_tile % PEER_KEYS == 0
    n_e = n_exp // e_tile
    c1, e1, r2, e2 = tabs
    nk = c1.shape[1]
    row = lambda i, j: (i, 0)
    cur = lambda j: jnp.minimum(j, n_e - 1)
    i1_spec = pl.BlockSpec((PEER_HEADS, e_tile // nk, tile), lambda i, j: (0, cur(j), i))
    i2_spec = pl.BlockSpec((PEER_HEADS, nk // 2, tile), lambda i, j: (0, 0, i))
    return pl.pallas_call(
        functools.partial(_peer_kernel, final=final),
        grid=(t // tile, n_e + 1),
        in_specs=[pl.BlockSpec((tile, d), row),
                  pl.BlockSpec((e_tile, d), lambda i, j: (cur(j), 0)),
                  pl.BlockSpec((d, e_tile), lambda i, j: (0, jnp.maximum(j - 1, 0))),
                  i1_spec, i1_spec, i2_spec, i2_spec,
                  pl.BlockSpec((tile, d), row),
                  _mod_spec(mod, 5, tile, tiles_per_seq),
                  pl.BlockSpec((1, d), lambda i, j: (0, 0))],
        out_specs=pl.BlockSpec((tile, d), row),
        out_shape=jax.ShapeDtypeStruct((t, d), F32),
        scratch_shapes=[pltpu.VMEM((d, tile), F32), pltpu.VMEM((2, e_tile, tile), BF16),
                        pltpu.VMEM((tile // LANES, PEER_KEYS, LANES), BF16),
                        pltpu.VMEM((tile // LANES, PEER_KEYS, LANES), BF16)],
        compiler_params=_cparams(("parallel", "arbitrary")),
        name="peer_dense",
    )(h2, u_tab, vt_tab, c1, e1, r2, e2, x_new, mod, g_final)


def _token_tile(seq, want):
    tile = min(want, seq)
    assert seq % tile == 0
    return tile


def _layer(x, mod, past, conv_prev, weights, batch, seq, layer, final):
    (g1, w_in, conv_w, g_attn, g_conv, w_o, g2, w_query, sub_keys, u_tab, vt_tab, g_final) = weights
    decode = past is not None
    t, d = x.shape
    if decode:
        tile, tps = t, 1
    else:
        tile = _token_tile(seq, 512)
        tps = seq // tile
    q0, q1, k, v, gb, u = _proj_in(x, mod, g1, w_in, tile, tps)
    cd = u.shape[1]
    if decode:
        cache_k, cache_v, page_table = past
        qs = jnp.concatenate([q0.reshape(batch, seq, KV_DIM), q1.reshape(batch, seq, KV_DIM)], axis=1)
        o = _moba_decode(qs, k.reshape(batch, seq, KV_DIM), v.reshape(batch, seq, KV_DIM),
                         cache_k, cache_v, page_table, layer)
        o = o.reshape(batch, KV_HEADS, Q_GROUP, seq, KV_HEADS, HEAD_DIM)
        o = jnp.stack([o[:, hk, :, :, hk, :] for hk in range(KV_HEADS)], axis=1)
        attn = o.transpose(0, 3, 1, 2, 4).reshape(t, ATTN_DIM)
        xp = jnp.concatenate([conv_prev, u.reshape(batch, seq, cd)], axis=1)
        shifted = (xp[:, 1:1 + seq].reshape(t, cd), xp[:, 0:seq].reshape(t, cd))
        conv_new = xp[:, -(CONV_WIDTH - 1):]
    else:
        attn = _moba_prompt(q0, q1, k, v, batch, seq)
        shifted = None
        conv_new = u.reshape(batch, seq, cd)[:, -(CONV_WIDTH - 1):]
    x_new, h2, scores = _mix(x, attn, gb, u, shifted, mod, conv_w, g_attn, g_conv, w_o, g2,
                             w_query, sub_keys, tile, tps)
    tabs = _route(scores)
    x_out = _peer(h2, u_tab, vt_tab, tabs, x_new, mod, g_final, tile, tps, final)
    return x_out, k, v, conv_new


def kernel(x_prompt, x_sample, cache_k, cache_v, state_conv, page_table, c_prompt, c_sample,
           w_ada, b_ada, g_norm1, g_norm2, w_in, conv_w, g_out_attn, g_out_conv, w_o,
           w_query, sub_keys, u_experts, v_experts, g_final):
    b, s, d = x_prompt.shape
    bd, sd, _ = x_sample.shape
    depth = w_ada.shape[0]
    assert s % MOBA_BLOCK == 0 and (bd * sd) % LANES == 0

    mod = _ada_mod(jnp.concatenate([c_prompt, c_sample], axis=0), w_ada, b_ada)

    scale = HEAD_DIM ** -0.5
    cols0 = jnp.concatenate([jnp.arange(HEAD_DIM) + (hk * Q_GROUP) * HEAD_DIM for hk in range(KV_HEADS)])
    cols1 = cols0 + HEAD_DIM

    xp = x_prompt.reshape(b * s, d)
    xs = x_sample.reshape(bd * sd, d)
    zero_conv = jnp.zeros((b, CONV_WIDTH - 1, conv_w.shape[-1]), x_prompt.dtype)
    ks_p, vs_p, cs_p, ks_s, vs_s, cs_s = [], [], [], [], [], []
    for l in range(depth):
        wl = w_in[l]
        w_in_l = jnp.concatenate([wl[:, cols0] * scale, wl[:, cols1] * scale, wl[:, ATTN_DIM:]],
                                 axis=1).astype(BF16)
        weights = (g_norm1[l][None], w_in_l, conv_w[l], g_out_attn[l][None], g_out_conv[l][None],
                   w_o[l].astype(BF16), g_norm2[l][None], w_query[l], sub_keys[l],
                   u_experts[l].astype(BF16), v_experts[l].T.astype(BF16), g_final[None])
        final = l == depth - 1
        mod_p = mod[l, :b][:, None, :]
        mod_s = jnp.repeat(mod[l, b:], sd, axis=0)
        xp, k, v, cn = _layer(xp, mod_p, None, zero_conv, weights, b, s, l, final)
        ks_p.append(k.reshape(b, s, KV_HEADS, HEAD_DIM))
        vs_p.append(v.reshape(b, s, KV_HEADS, HEAD_DIM))
        cs_p.append(cn)
        xs, k, v, cn = _layer(xs, mod_s, (cache_k, cache_v, page_table), state_conv[l], weights,
                              bd, sd, l, final)
        ks_s.append(k.reshape(bd, sd, KV_HEADS, HEAD_DIM))
        vs_s.append(v.reshape(bd, sd, KV_HEADS, HEAD_DIM))
        cs_s.append(cn)
    return (xp.reshape(b, s, d), xs.reshape(bd, sd, d),
            jnp.stack(ks_p), jnp.stack(vs_p), jnp.stack(cs_p),
            jnp.stack(ks_s), jnp.stack(vs_s), jnp.stack(cs_s))
```

```python
import functools

import jax
import jax.numpy as jnp
from jax import lax
from jax.experimental import pallas as pl
from jax.experimental.pallas import tpu as pltpu

F32 = jnp.float32
BF16 = jnp.bfloat16

N_HEADS = 8
KV_HEADS = 4
HEAD_DIM = 64
Q_GROUP = N_HEADS // KV_HEADS
ATTN_DIM = N_HEADS * HEAD_DIM
KV_DIM = KV_HEADS * HEAD_DIM
MOBA_BLOCK = 256
MOBA_TOPK = 3
Q_CHUNK = 128
CONV_WIDTH = 3
PEER_HEADS = 8
PEER_KEYS = 128
PEER_HALF = 64
PEER_TOPK = 16
NORM_EPS = 1e-6
NEG_BIG = -1e30
POS_BIG = 1e30

VMEM_LIMIT_BYTES = 56 * 1024 * 1024
LANES = 128
BF16_ROWS = 16

NT_DIMS = (((1,), (1,)), ((), ()))


def _cparams(sem):
    return pltpu.CompilerParams(dimension_semantics=sem, vmem_limit_bytes=VMEM_LIMIT_BYTES)


def _rms(x, g):
    y = x * lax.rsqrt(jnp.mean(x * x, axis=-1, keepdims=True) + NORM_EPS)
    return y * g


def _alibi_slope(h):
    return 2.0 ** (-8.0 * (h + 1) / N_HEADS)


def _ada_kernel(c_ref, w_ref, b_ref, o_ref):
    c = c_ref[...]
    o_ref[...] = jnp.dot(c * jax.nn.sigmoid(c), w_ref[...], precision=lax.Precision.HIGHEST,
                         preferred_element_type=F32) + b_ref[...]


def _ada_mod(c_all, w_ada, b_ada):
    depth, d, d6 = w_ada.shape
    n = c_all.shape[0]
    return pl.pallas_call(
        _ada_kernel,
        grid=(depth, d6 // d),
        in_specs=[pl.BlockSpec((n, d), lambda l, j: (0, 0)),
                  pl.BlockSpec((None, d, d), lambda l, j: (l, 0, j)),
                  pl.BlockSpec((None, 1, d), lambda l, j: (l, 0, j))],
        out_specs=pl.BlockSpec((None, n, d), lambda l, j: (l, 0, j)),
        out_shape=jax.ShapeDtypeStruct((depth, n, d6), F32),
        compiler_params=_cparams(("parallel", "parallel")),
        name="ada_mod",
    )(c_all, w_ada, b_ada.reshape(depth, 1, d6))


def _mod_spec(mod, col, tile, tiles_per_seq):
    d = mod.shape[-1] // 6
    if mod.ndim == 3:
        return pl.BlockSpec((None, 1, d), lambda i, *_: (i // tiles_per_seq, 0, col))
    return pl.BlockSpec((tile, d), lambda i, *_: (i, col))


def _proj_kernel(x_ref, sh_ref, sc_ref, g_ref, w_ref, q0_ref, q1_ref, k_ref, v_ref, gb_ref, u_ref):
    h = _rms(x_ref[...], g_ref[...]) * (1.0 + sc_ref[...]) + sh_ref[...]
    p = jnp.dot(h.astype(BF16), w_ref[...], preferred_element_type=F32)
    q0_ref[...] = p[:, 0:KV_DIM]
    q1_ref[...] = p[:, KV_DIM:2 * KV_DIM]
    k_ref[...] = p[:, 2 * KV_DIM:3 * KV_DIM]
    v_ref[...] = p[:, 3 * KV_DIM:4 * KV_DIM]
    c0 = 4 * KV_DIM
    cd = (p.shape[1] - c0) // 3
    gb_ref[...] = p[:, c0:c0 + cd]
    u_ref[...] = p[:, c0 + cd:c0 + 2 * cd] * p[:, c0 + 2 * cd:c0 + 3 * cd]


def _proj_in(x, mod, g1, w_in, tile, tiles_per_seq):
    t, d = x.shape
    n_in = w_in.shape[1]
    cd = (n_in - 4 * KV_DIM) // 3
    row = lambda i: (i, 0)
    outs = [jax.ShapeDtypeStruct((t, KV_DIM), F32)] * 4 + [jax.ShapeDtypeStruct((t, cd), F32)] * 2
    out_specs = [pl.BlockSpec((tile, KV_DIM), row)] * 4 + [pl.BlockSpec((tile, cd), row)] * 2
    return pl.pallas_call(
        _proj_kernel,
        grid=(t // tile,),
        in_specs=[pl.BlockSpec((tile, d), row),
                  _mod_spec(mod, 0, tile, tiles_per_seq),
                  _mod_spec(mod, 1, tile, tiles_per_seq),
                  pl.BlockSpec((1, d), lambda i: (0, 0)),
                  pl.BlockSpec((d, n_in), lambda i: (0, 0))],
        out_specs=out_specs,
        out_shape=outs,
        compiler_params=_cparams(("parallel",)),
        name="proj_in",
    )(x, mod, mod, g1, w_in)


def _moba_prompt_kernel(q0_ref, q1_ref, k_ref, v_ref, o_ref, kb_s, vb_s, km4_s, e_s):
    c = pl.program_id(1)
    s_len = k_ref.shape[0]
    nb = s_len // MOBA_BLOCK

    @pl.when(c == 0)
    def _():
        kf = k_ref[...]
        kb_s[...] = kf.astype(BF16)
        vb_s[...] = v_ref[...].astype(BF16)
        km = jnp.sum(kf.reshape(nb, MOBA_BLOCK, KV_DIM), axis=1) * (1.0 / MOBA_BLOCK)
        lane_head = lax.broadcasted_iota(jnp.int32, km.shape, 1) // HEAD_DIM
        parts = [jnp.where(lane_head == hk, km, 0.0) for hk in range(KV_HEADS)]
        parts.append(jnp.zeros((LANES - KV_HEADS * nb, KV_DIM), F32))
        km4_s[...] = jnp.concatenate(parts, axis=0).T
        key_blk = lax.broadcasted_iota(jnp.int32, (LANES, s_len), 1) // MOBA_BLOCK
        row = lax.broadcasted_iota(jnp.int32, (LANES, s_len), 0)
        for hk in range(KV_HEADS):
            e_s[hk] = (key_blk + hk * nb == row).astype(BF16)

    chunks_per_blk = MOBA_BLOCK // Q_CHUNK
    own = c // chunks_per_blk
    for n_past in range(nb):
        @pl.when(own == n_past)
        def _(n_past=n_past):
            _moba_prompt_chunk(n_past, (c % chunks_per_blk) * Q_CHUNK, q0_ref, q1_ref, o_ref,
                               kb_s, vb_s, km4_s, e_s)


def _block_penalty(q, km4, n_past, nb):
    gate = jnp.dot(q, km4, precision=lax.Precision.HIGHEST, preferred_element_type=F32)
    n_of = lax.broadcasted_iota(jnp.int32, gate.shape, 1) & (nb - 1)
    gate = jnp.where(n_of < n_past, gate, -jnp.inf)
    rank = jnp.zeros(gate.shape, F32)
    for delta in range(1, n_past):
        above = pltpu.roll(gate, LANES - delta, 1)
        rank = rank + jnp.where(above > gate, jnp.where(n_of + delta < nb, 1.0, 0.0), 0.0)
        below = pltpu.roll(gate, delta, 1)
        rank = rank + jnp.where(below >= gate, jnp.where(n_of >= delta, 1.0, 0.0), 0.0)
    return jnp.where(rank < float(MOBA_TOPK), 0.0, NEG_BIG).astype(BF16)


def _moba_prompt_chunk(n_past, q_off, q0_ref, q1_ref, o_ref, kb_s, vb_s, km4_s, e_s):
    nb = kb_s.shape[0] // MOBA_BLOCK
    n_keys = (n_past + 1) * MOBA_BLOCK
    past_keys = n_past * MOBA_BLOCK
    tk = lax.broadcasted_iota(jnp.int32, (1, n_keys), 1).astype(F32)
    q_pos = q_off + lax.broadcasted_iota(jnp.int32, (Q_CHUNK, MOBA_BLOCK), 0)
    causal = jnp.where(q_pos >= lax.broadcasted_iota(jnp.int32, (Q_CHUNK, MOBA_BLOCK), 1), 0.0, NEG_BIG)
    lane_head = lax.broadcasted_iota(jnp.int32, (Q_CHUNK, KV_DIM), 1) // HEAD_DIM
    need_gate = n_past > MOBA_TOPK
    qs = (q0_ref[...], q1_ref[...])
    if need_gate:
        km4 = km4_s[...]
        penalty = [_block_penalty(q, km4, n_past, nb) for q in qs]
    kb = kb_s[0:n_keys, :]
    vb = vb_s[0:n_keys, :]
    outs = []
    for h in range(N_HEADS):
        hk, g = divmod(h, Q_GROUP)
        qpad = jnp.where(lane_head == hk, qs[g], 0.0)
        s = lax.dot_general(qpad.astype(BF16), kb, NT_DIMS, preferred_element_type=F32)
        st = s + _alibi_slope(h) * tk
        if need_gate:
            bias = jnp.dot(penalty[g], e_s[hk, :, 0:past_keys], preferred_element_type=F32)
            st = jnp.concatenate([st[:, :past_keys] + bias, st[:, past_keys:] + causal], axis=1)
        elif n_past > 0:
            st = jnp.concatenate([st[:, :past_keys], st[:, past_keys:] + causal], axis=1)
        else:
            st = st + causal
        mx = jnp.max(st, axis=1, keepdims=True)
        p = jnp.exp(st - mx)
        l = jnp.sum(p, axis=1, keepdims=True)
        o = jnp.dot(p.astype(BF16), vb, preferred_element_type=F32)
        outs.append(o[:, hk * HEAD_DIM:(hk + 1) * HEAD_DIM] / l)
    o_ref[...] = jnp.concatenate(outs, axis=1)


def _moba_prompt(q0, q1, k, v, batch, seq):
    n_chunks = seq // Q_CHUNK
    nb = seq // MOBA_BLOCK
    assert nb & (nb - 1) == 0 and KV_HEADS * nb <= LANES
    qspec = pl.BlockSpec((Q_CHUNK, KV_DIM), lambda b, c: (b * n_chunks + c, 0))
    kvspec = pl.BlockSpec((seq, KV_DIM), lambda b, c: (b, 0))
    return pl.pallas_call(
        _moba_prompt_kernel,
        grid=(batch, n_chunks),
        in_specs=[qspec, qspec, kvspec, kvspec],
        out_specs=pl.BlockSpec((Q_CHUNK, ATTN_DIM), lambda b, c: (b * n_chunks + c, 0)),
        out_shape=jax.ShapeDtypeStruct((batch * seq, ATTN_DIM), F32),
        scratch_shapes=[pltpu.VMEM((seq, KV_DIM), BF16), pltpu.VMEM((seq, KV_DIM), BF16),
                        pltpu.VMEM((KV_DIM, LANES), F32), pltpu.VMEM((KV_HEADS, LANES, seq), BF16)],
        compiler_params=_cparams(("parallel", "arbitrary")),
        name="moba_prompt",
    )(q0, q1, k, v)


def _moba_decode_kernel(pt_ref, qs_ref, kn_ref, vn_ref, *refs, pps, n_blk, pos0, page):
    del pt_ref
    kp = refs[:pps]
    vp = refs[pps:2 * pps]
    o_ref = refs[2 * pps]
    m_s, l_s, km_s, o_s = refs[2 * pps + 1:]
    step = pl.program_id(1)
    n_rows = KV_HEADS * qs_ref.shape[0]
    sd = qs_ref.shape[0] // Q_GROUP
    ppb = MOBA_BLOCK // page
    bps = pps // ppb

    qs = qs_ref[...]
    lane_head = lax.broadcasted_iota(jnp.int32, qs.shape, 1) // HEAD_DIM
    qpad = jnp.concatenate([jnp.where(lane_head == hk, qs, 0.0) for hk in range(KV_HEADS)], axis=0)
    qpad_b = qpad.astype(BF16)
    row = lax.broadcasted_iota(jnp.int32, (n_rows, 1), 0)
    head = (row // qs.shape[0]) * Q_GROUP + (row % qs.shape[0]) // sd
    qi = row % sd
    slope = jnp.zeros((n_rows, 1), F32)
    for h in range(N_HEADS):
        slope = jnp.where(head == h, _alibi_slope(h), slope)
    lane = lax.broadcasted_iota(jnp.int32, (n_rows, LANES), 1)

    @pl.when(step == 0)
    def _():
        m_s[...] = jnp.zeros_like(m_s)
        l_s[...] = jnp.zeros_like(l_s)
        km_s[...] = jnp.zeros_like(km_s)

    kcol = lax.broadcasted_iota(jnp.int32, (KV_DIM, LANES), 1)
    for n in range(bps):
        blk = step * bps + n
        sts = []
        ksum = jnp.zeros((KV_DIM, 1), F32)
        for i in range(ppb):
            kt = kp[n * ppb + i][...]
            s = jnp.dot(qpad_b, kt.astype(BF16), preferred_element_type=F32)
            tk = blk * MOBA_BLOCK + i * page + lax.broadcasted_iota(jnp.int32, (1, page), 1)
            sts.append(s - slope * ((pos0 + qi) - tk).astype(F32))
            ksum = ksum + jnp.sum(kt, axis=1, keepdims=True)
        mx = sts[0].max(axis=1, keepdims=True)
        for st in sts[1:]:
            mx = jnp.maximum(mx, st.max(axis=1, keepdims=True))
        l = jnp.zeros_like(mx)
        o = jnp.zeros((n_rows, KV_DIM), F32)
        for i, st in enumerate(sts):
            p = jnp.exp(st - mx)
            l = l + jnp.sum(p, axis=1, keepdims=True)
            o = o + lax.dot_general(p.astype(BF16), vp[n * ppb + i][...].astype(BF16), NT_DIMS,
                                    preferred_element_type=F32)
        o_s[blk] = o
        here = lane == blk
        m_s[...] = jnp.where(here, mx, m_s[...])
        l_s[...] = jnp.where(here, l, l_s[...])
        km_s[...] = jnp.where(kcol == blk, ksum * (1.0 / MOBA_BLOCK), km_s[...])

    @pl.when(step == pl.num_programs(1) - 1)
    def _():
        valid = lane < n_blk
        gates = jnp.dot(qpad, km_s[...], precision=lax.Precision.HIGHEST, preferred_element_type=F32)
        work = jnp.where(valid, gates, -jnp.inf)
        sel = jnp.zeros((n_rows, LANES), F32)
        for _ in range(min(MOBA_TOPK, n_blk)):
            top = jnp.max(work, axis=1, keepdims=True)
            first = jnp.min(jnp.where(work == top, lane, LANES), axis=1, keepdims=True)
            hit = (lane == first) & valid
            sel = jnp.where(hit, 1.0, sel)
            work = jnp.where(hit, -jnp.inf, work)
        picked = sel > 0.0
        own_scores = []
        for j in range(sd):
            sj = jnp.sum(qpad * kn_ref[j:j + 1, :], axis=1, keepdims=True)
            dj = (qi - j).astype(F32)
            own_scores.append(jnp.where(dj >= 0.0, sj - slope * dj, NEG_BIG))
        m_tot = jnp.max(jnp.where(picked, m_s[...], NEG_BIG), axis=1, keepdims=True)
        for sj in own_scores:
            m_tot = jnp.maximum(m_tot, sj)
        wn = jnp.where(picked, jnp.exp(m_s[...] - m_tot), 0.0)
        denom = jnp.sum(wn * l_s[...], axis=1, keepdims=True)
        acc = jnp.zeros((n_rows, KV_DIM), F32)
        for j, sj in enumerate(own_scores):
            pj = jnp.exp(sj - m_tot)
            denom = denom + pj
            acc = acc + pj * vn_ref[j:j + 1, :]
        for n in range(n_blk):
            coef = jnp.sum(jnp.where(lane == n, wn, 0.0), axis=1, keepdims=True)
            acc = acc + coef * o_s[n]
        o_ref[...] = acc / denom


def _moba_decode(qs, k_new, v_new, cache_k, cache_v, page_table, layer):
    bd, rows8, _ = qs.shape
    sd = k_new.shape[1]
    n_pages = page_table.shape[1]
    page = cache_k.shape[2]
    n_pool = cache_k.shape[1]
    ppb = MOBA_BLOCK // page
    n_blk = n_pages // ppb
    assert n_pages % ppb == 0 and n_blk <= LANES
    pps = min(16, n_pages)
    assert n_pages % pps == 0 and pps % ppb == 0
    ck = cache_k.transpose(0, 1, 3, 4, 2).reshape(cache_k.shape[0], n_pool, KV_DIM, page)
    cv = cache_v.transpose(0, 1, 3, 4, 2).reshape(cache_v.shape[0], n_pool, KV_DIM, page)
    n_rows = KV_HEADS * rows8

    def page_spec(i):
        return pl.BlockSpec((None, None, KV_DIM, page),
                            lambda b, s, pt: (layer, pt[b, s * pps + i], 0, 0))

    seq_spec = lambda r: pl.BlockSpec((None, r, KV_DIM), lambda b, s, pt: (b, 0, 0))
    grid_spec = pltpu.PrefetchScalarGridSpec(
        num_scalar_prefetch=1,
        grid=(bd, n_pages // pps),
        in_specs=[seq_spec(rows8), seq_spec(sd), seq_spec(sd)]
        + [page_spec(i) for i in range(pps)] * 2,
        out_specs=seq_spec(n_rows),
        scratch_shapes=[pltpu.VMEM((n_rows, LANES), F32)] * 2
        + [pltpu.VMEM((KV_DIM, LANES), F32), pltpu.VMEM((n_blk, n_rows, KV_DIM), F32)],
    )
    kern = functools.partial(_moba_decode_kernel, pps=pps, n_blk=n_blk,
                             pos0=n_pages * page, page=page)
    return pl.pallas_call(
        kern,
        grid_spec=grid_spec,
        out_shape=jax.ShapeDtypeStruct((bd, n_rows, KV_DIM), F32),
        compiler_params=_cparams(("parallel", "arbitrary")),
        name="moba_decode",
    )(page_table, qs, k_new, v_new, *([ck] * pps), *([cv] * pps))


def _mix_kernel(*refs, decode, tiles_per_seq):
    if decode:
        (x_ref, attn_ref, gb_ref, u_ref, u1_ref, u2_ref, ga1_ref, sh2_ref, sc2_ref, cw_ref, ga_ref,
         gc_ref, wo_ref, g2_ref, wq_ref, sk_ref, xo_ref, h2_ref, st_ref) = refs
        u0 = u_ref[...]
        u1 = u1_ref[...]
        u2 = u2_ref[...]
    else:
        (x_ref, attn_ref, gb_ref, u_ref, halo_ref, ga1_ref, sh2_ref, sc2_ref, cw_ref, ga_ref,
         gc_ref, wo_ref, g2_ref, wq_ref, sk_ref, xo_ref, h2_ref, st_ref) = refs
        u0 = u_ref[...]
        first = pl.program_id(0) % tiles_per_seq == 0
        halo = jnp.where(first, 0.0, halo_ref[...])
        hr = halo.shape[0]
        rowi = lax.broadcasted_iota(jnp.int32, u0.shape, 0)
        u1 = jnp.where(rowi < 1, halo[hr - 1:hr, :], pltpu.roll(u0, 1, 0))
        u2 = jnp.where(rowi < 1, halo[hr - 2:hr - 1, :],
                       jnp.where(rowi < 2, halo[hr - 1:hr, :], pltpu.roll(u0, 2, 0)))
    cw = cw_ref[...]
    y = cw[0:1, :] * u2
    y = y + cw[1:2, :] * u1
    y = y + cw[2:3, :] * u0
    yc = gb_ref[...] * y
    cat = jnp.concatenate([_rms(attn_ref[...], ga_ref[...]), _rms(yc, gc_ref[...])], axis=1)
    mixed = jnp.dot(cat.astype(BF16), wo_ref[...], preferred_element_type=F32)
    x_new = x_ref[...] + ga1_ref[...] * mixed
    xo_ref[...] = x_new
    h2 = _rms(x_new, g2_ref[...]) * (1.0 + sc2_ref[...]) + sh2_ref[...]
    h_hi = h2.astype(BF16)
    h2_ref[...] = h_hi
    h_lo = (h2 - h_hi.astype(F32)).astype(BF16)
    q = jnp.dot(h_hi, wq_ref[0], preferred_element_type=F32) + (
        jnp.dot(h_hi, wq_ref[1], preferred_element_type=F32)
        + jnp.dot(h_lo, wq_ref[0], preferred_element_type=F32))
    qt = q.T
    for h in range(PEER_HEADS):
        for p in range(2):
            r0 = (h * 2 + p) * PEER_HALF
            st_ref[h, p] = jnp.dot(sk_ref[p], qt[r0:r0 + PEER_HALF, :],
                                   precision=lax.Precision.HIGHEST, preferred_element_type=F32)


def _mix(x, attn, gb, u, shifted, mod, conv_w, g_attn, g_conv, w_o, g2, w_query, sub_keys,
         tile, tiles_per_seq):
    t, d = x.shape
    cd = u.shape[1]
    decode = shifted is not None
    row = lambda i: (i, 0)
    full2 = lambda a: pl.BlockSpec(a.shape, lambda i: (0,) * a.ndim)
    if decode:
        conv_in = [u, shifted[0], shifted[1]]
        conv_specs = [pl.BlockSpec((tile, cd), row)] * 3
    else:
        halo_rows = 8
        conv_in = [u, u]
        conv_specs = [pl.BlockSpec((tile, cd), row),
                      pl.BlockSpec((halo_rows, cd),
                                   lambda i: (jnp.maximum(i * (tile // halo_rows) - 1, 0), 0))]
    kern = functools.partial(_mix_kernel, decode=decode, tiles_per_seq=tiles_per_seq)
    return pl.pallas_call(
        kern,
        grid=(t // tile,),
        in_specs=[pl.BlockSpec((tile, d), row), pl.BlockSpec((tile, attn.shape[1]), row),
                  pl.BlockSpec((tile, cd), row)] + conv_specs
        + [_mod_spec(mod, 2, tile, tiles_per_seq), _mod_spec(mod, 3, tile, tiles_per_seq),
           _mod_spec(mod, 4, tile, tiles_per_seq),
           full2(conv_w), full2(g_attn), full2(g_conv), full2(w_o), full2(g2), full2(w_query),
           full2(sub_keys)],
        out_specs=[pl.BlockSpec((tile, d), row), pl.BlockSpec((tile, d), row),
                   pl.BlockSpec((PEER_HEADS, 2, PEER_KEYS, tile), lambda i: (0, 0, 0, i))],
        out_shape=[jax.ShapeDtypeStruct((t, d), F32), jax.ShapeDtypeStruct((t, d), BF16),
                   jax.ShapeDtypeStruct((PEER_HEADS, 2, PEER_KEYS, t), F32)],
        compiler_params=_cparams(("parallel",)),
        name="mix_scores",
    )(x, attn, gb, *conv_in, mod, mod, mod, conv_w, g_attn, g_conv, w_o, g2, w_query, sub_keys)


def _top_rows(work, n_take, vals_ref):
    n = work.shape[0]
    idx = lax.broadcasted_iota(jnp.int32, work.shape, 0).astype(F32)
    rank = jnp.full(work.shape, float(n_take), F32)
    for r in range(n_take):
        top = jnp.max(work, axis=0, keepdims=True)
        first = jnp.min(jnp.where(work == top, idx, float(n)), axis=0, keepdims=True)
        hit = idx == first
        rank = jnp.where(hit, float(r), rank)
        work = jnp.where(hit, -jnp.inf, work)
        if vals_ref is not None:
            vals_ref[r:r + 1, :] = top
    return rank


def _top_rows_distinct(work, n_take, vals_ref):
    rank = jnp.full(work.shape, float(n_take), F32)
    for r in range(n_take):
        top = jnp.max(work, axis=0, keepdims=True)
        hit = work == top
        rank = jnp.where(hit, float(r), rank)
        work = jnp.where(hit, -jnp.inf, work)
        if vals_ref is not None:
            vals_ref[r:r + 1, :] = top
    n_marked = jnp.sum(jnp.where(rank < float(n_take), 1.0, 0.0), axis=0, keepdims=True)
    return rank, n_marked


def _dup_bf16(x):
    u = lax.bitcast_convert_type(x, jnp.uint32)
    r = (u + jnp.uint32(0x7FFF) + ((u >> 16) & jnp.uint32(1))) >> 16
    return r | (r << 16)


def _write_tables(h, refs, s1, s2, rank1, rank2, c1, z, a0, b0):
    c1_ref, e1_ref, r2_ref, e2_ref = refs
    c1_ref[h] = _dup_bf16(c1)
    e1_ref[h] = _dup_bf16(jnp.exp(s1 - a0) / z)
    r2_ref[h] = pltpu.bitcast(rank2.astype(BF16), jnp.uint32)
    e2_ref[h] = pltpu.bitcast(jnp.exp(s2 - b0).astype(BF16), jnp.uint32)


def _route_head_exact(h, s_ref, refs, a_s, b_s):
    s1 = s_ref[h, 0]
    s2 = s_ref[h, 1]
    rank1 = _top_rows(s1, PEER_TOPK, a_s)
    rank2 = _top_rows(s2, PEER_TOPK, b_s)
    a = a_s[...]
    b = b_s[...]
    cand = jnp.concatenate([a[r:r + 1, :] + b for r in range(PEER_TOPK)], axis=0)
    crank = _top_rows(cand, PEER_TOPK, None)
    sel = crank < float(PEER_TOPK)
    top0 = a[0:1, :] + b[0:1, :]
    z = jnp.sum(jnp.where(sel, jnp.exp(cand - top0), 0.0), axis=0, keepdims=True)
    self = jnp.where(sel, 1.0, 0.0)
    c1 = jnp.zeros_like(s1)
    for r in range(PEER_TOPK):
        cnt = jnp.sum(self[r * PEER_TOPK:(r + 1) * PEER_TOPK, :], axis=0, keepdims=True)
        c1 = jnp.where(rank1 == float(r), cnt, c1)
    _write_tables(h, refs, s1, s2, rank1, rank2, c1, z, a[0:1, :], b[0:1, :])


def _route_head_distinct(h, s_ref, refs, a_s, b_s):
    s1 = s_ref[h, 0]
    s2 = s_ref[h, 1]
    rank1, n1 = _top_rows_distinct(s1, PEER_TOPK, a_s)
    rank2, n2 = _top_rows_distinct(s2, PEER_TOPK, b_s)
    a = a_s[...]
    b = b_s[...]
    half = PEER_TOPK // 2
    cand = jnp.concatenate([a[0:1, :] + b] + [a[r:r + 1, :] + b[0:half, :] for r in range(1, half)]
                           + [a[half:, :] + b[0:1, :]], axis=0)
    row = lax.broadcasted_iota(jnp.int32, cand.shape, 0)
    assert half == 8
    r_mid = ((row - PEER_TOPK) >> 3) + 1
    c_mid = (row - PEER_TOPK) & 7
    in_mid = (row >= PEER_TOPK) & (row < PEER_TOPK + half * (half - 1))
    cand = jnp.where(in_mid & ((r_mid + 1) * (c_mid + 1) > PEER_TOPK), -jnp.inf, cand)
    crank, nc = _top_rows_distinct(cand, PEER_TOPK, None)
    sel = crank < float(PEER_TOPK)
    top0 = a[0:1, :] + b[0:1, :]
    z = jnp.sum(jnp.where(sel, jnp.exp(cand - top0), 0.0), axis=0, keepdims=True)
    self = jnp.where(sel, 1.0, 0.0)
    cnts = [jnp.sum(self[0:PEER_TOPK, :], axis=0, keepdims=True)]
    for r in range(1, half):
        lo = PEER_TOPK + (r - 1) * half
        cnts.append(jnp.sum(self[lo:lo + half, :], axis=0, keepdims=True))
    tail = PEER_TOPK + half * (half - 1)
    cnts += [self[tail + r:tail + r + 1, :] for r in range(half)]
    c1 = jnp.zeros_like(s1)
    for r in range(PEER_TOPK):
        c1 = jnp.where(rank1 == float(r), cnts[r], c1)
    _write_tables(h, refs, s1, s2, rank1, rank2, c1, z, a[0:1, :], b[0:1, :])
    want = float(PEER_TOPK)
    return jnp.abs(n1 - want) + jnp.abs(n2 - want) + jnp.abs(nc - want)


def _route_kernel(s_ref, c1_ref, e1_ref, r2_ref, e2_ref, a_s, b_s):
    refs = (c1_ref, e1_ref, r2_ref, e2_ref)

    def head_pair(i, carry):
        heads = (2 * i, 2 * i + 1)
        ties = [_route_head_distinct(h, s_ref, refs, a_s.at[k], b_s.at[k]) for k, h in enumerate(heads)]
        for k, h in enumerate(heads):
            @pl.when(jnp.max(ties[k]) > 0.0)
            def _(k=k, h=h):
                _route_head_exact(h, s_ref, refs, a_s.at[k], b_s.at[k])

        return carry

    lax.fori_loop(0, PEER_HEADS // 2, head_pair, 0)


def _route(scores):
    _, _, nk, t = scores.shape
    tile = LANES
    wspec = pl.BlockSpec((PEER_HEADS, nk, tile), lambda i: (0, 0, i))
    pspec = pl.BlockSpec((PEER_HEADS, nk // 2, tile), lambda i: (0, 0, i))
    words = jax.ShapeDtypeStruct((PEER_HEADS, nk, t), jnp.uint32)
    packed = jax.ShapeDtypeStruct((PEER_HEADS, nk // 2, t), jnp.uint32)
    return pl.pallas_call(
        _route_kernel,
        grid=(t // tile,),
        in_specs=[pl.BlockSpec((PEER_HEADS, 2, nk, tile), lambda i: (0, 0, 0, i))],
        out_specs=[wspec, wspec, pspec, pspec],
        out_shape=[words, words, packed, packed],
        scratch_shapes=[pltpu.VMEM((2, PEER_TOPK, tile), F32)] * 2,
        compiler_params=_cparams(("parallel",)),
        name="peer_route",
    )(scores)


def _gelu(x):
    return 0.5 * x * (1.0 + lax.erf(x * (2.0 ** -0.5)))


def _peer_kernel(h2_ref, u_ref, vt_ref, c1_ref, e1_ref, r2_ref, e2_ref, x_ref, ga2_ref, gf_ref,
                 o_ref, acc_s, w_a, w_b, ht_a, ht_b, *, final):
    j = pl.program_id(1)
    n_e = pl.num_programs(1) - 1
    slot = j % 2

    @pl.when(j == 0)
    def _():
        acc_s[...] = jnp.zeros_like(acc_s)
        w_b[...] = jnp.zeros_like(w_b)

    def expert_tile(w_prev, w_cur):
        xb = h2_ref[...]
        tt = xb.shape[0]
        grp = PEER_KEYS // BF16_ROWS
        pair = 2 * PEER_KEYS
        n_stages = u_ref.shape[0] // pair
        hts = (ht_a, ht_b)

        def scores(k):
            rows = slice(k * pair, (k + 1) * pair)
            hts[k % 2][...] = lax.dot_general(u_ref[rows, :], xb, NT_DIMS, preferred_element_type=F32)

        def second_product(k):
            rows = slice(k * pair, (k + 1) * pair)
            acc_s[rows, :] += jnp.dot(vt_ref[rows, :], w_prev[...], preferred_element_type=F32)

        def activations(k):
            ht_ref = hts[k % 2]
            c1_rows = [[pltpu.bitcast(jnp.broadcast_to(c1_ref[h, 2 * k + sub:2 * k + sub + 1, :], (8, tt)), BF16)
                        for h in range(PEER_HEADS)] for sub in range(2)]
            e1_rows = [[pltpu.bitcast(jnp.broadcast_to(e1_ref[h, 2 * k + sub:2 * k + sub + 1, :], (8, tt)), BF16)
                        for h in range(PEER_HEADS)] for sub in range(2)]
            for cb in range(tt // LANES):
                cols = slice(cb * LANES, (cb + 1) * LANES)
                gates = [jnp.zeros((PEER_KEYS, LANES), BF16) for _ in range(2)]
                for h in range(PEER_HEADS):
                    r2 = pltpu.bitcast(r2_ref[h, :, cols], BF16)
                    e2 = pltpu.bitcast(e2_ref[h, :, cols], BF16)
                    for sub in range(2):
                        c1 = jnp.concatenate([c1_rows[sub][h][:, cols]] * grp, axis=0)
                        e1 = jnp.concatenate([e1_rows[sub][h][:, cols]] * grp, axis=0)
                        gates[sub] = gates[sub] + jnp.where(r2 < c1, e2, jnp.zeros((), BF16)) * e1
                for sub in range(2):
                    r0 = k * pair + sub * PEER_KEYS
                    act = _gelu(ht_ref[sub * PEER_KEYS:(sub + 1) * PEER_KEYS, cols]).astype(BF16)
                    w_cur[r0:r0 + PEER_KEYS, cols] = act * gates[sub]

        scores(0)
        for k in range(1, n_stages):
            activations(k - 1)
            scores(k)
            second_product(k - 1)
        activations(n_stages - 1)
        second_product(n_stages - 1)

    @pl.when(slot == 0)
    def _():
        expert_tile(w_b, w_a)

    @pl.when(slot == 1)
    def _():
        expert_tile(w_a, w_b)

    @pl.when(j == n_e)
    def _():
        out = x_ref[...] + ga2_ref[...] * acc_s[...].T
        if final:
            out = _rms(out, gf_ref[...])
        o_ref[...] = out


def _peer(h2, u_tab, vt_tab, tabs, x_new, mod, g_final, tile, tiles_per_seq, final):
    t, d = x_new.shape
    n_exp = u_tab.shape[0]
    e_tile = d
    assert n_exp % e_tile == 0 and e_tile % (2 * PEER_KEYS) == 0
    n_e = n_exp // e_tile
    c1, e1, r2, e2 = tabs
    nk = c1.shape[1]
    row = lambda i, j: (i, 0)
    cur = lambda j: jnp.minimum(j, n_e - 1)
    i1_spec = pl.BlockSpec((PEER_HEADS, e_tile // nk, tile), lambda i, j: (0, cur(j), i))
    i2_spec = pl.BlockSpec((PEER_HEADS, nk // 2, tile), lambda i, j: (0, 0, i))
    return pl.pallas_call(
        functools.partial(_peer_kernel, final=final),
        grid=(t // tile, n_e + 1),
        in_specs=[pl.BlockSpec((tile, d), row),
                  pl.BlockSpec((e_tile, d), lambda i, j: (cur(j), 0)),
                  pl.BlockSpec((d, e_tile), lambda i, j: (0, jnp.maximum(j - 1, 0))),
                  i1_spec, i1_spec, i2_spec, i2_spec,
                  pl.BlockSpec((tile, d), row),
                  _mod_spec(mod, 5, tile, tiles_per_seq),
                  pl.BlockSpec((1, d), lambda i, j: (0, 0))],
        out_specs=pl.BlockSpec((tile, d), row),
        out_shape=jax.ShapeDtypeStruct((t, d), F32),
        scratch_shapes=[pltpu.VMEM((d, tile), F32),
                        pltpu.VMEM((e_tile, tile), BF16), pltpu.VMEM((e_tile, tile), BF16),
                        pltpu.VMEM((2 * PEER_KEYS, tile), F32), pltpu.VMEM((2 * PEER_KEYS, tile), F32)],
        compiler_params=_cparams(("parallel", "arbitrary")),
        name="peer_dense",
    )(h2, u_tab, vt_tab, c1, e1, r2, e2, x_new, mod, g_final)


def _split_bf16(w):
    hi = w.astype(BF16)
    return jnp.stack([hi, (w - hi.astype(F32)).astype(BF16)])


def _token_tile(seq, want):
    tile = min(want, seq)
    assert seq % tile == 0
    return tile


def _layer(x, mod, past, conv_prev, weights, batch, seq, layer, final):
    (g1, w_in, conv_w, g_attn, g_conv, w_o, g2, w_query, sub_keys, u_tab, vt_tab, g_final) = weights
    decode = past is not None
    t, d = x.shape
    if decode:
        tile, tps = t, 1
    else:
        tile = _token_tile(seq, 512)
        tps = seq // tile
    q0, q1, k, v, gb, u = _proj_in(x, mod, g1, w_in, tile, tps)
    cd = u.shape[1]
    if decode:
        cache_k, cache_v, page_table = past
        qs = jnp.concatenate([q0.reshape(batch, seq, KV_DIM), q1.reshape(batch, seq, KV_DIM)], axis=1)
        o = _moba_decode(qs, k.reshape(batch, seq, KV_DIM), v.reshape(batch, seq, KV_DIM),
                         cache_k, cache_v, page_table, layer)
        o = o.reshape(batch, KV_HEADS, Q_GROUP, seq, KV_HEADS, HEAD_DIM)
        o = jnp.stack([o[:, hk, :, :, hk, :] for hk in range(KV_HEADS)], axis=1)
        attn = o.transpose(0, 3, 1, 2, 4).reshape(t, ATTN_DIM)
        xp = jnp.concatenate([conv_prev, u.reshape(batch, seq, cd)], axis=1)
        shifted = (xp[:, 1:1 + seq].reshape(t, cd), xp[:, 0:seq].reshape(t, cd))
        conv_new = xp[:, -(CONV_WIDTH - 1):]
    else:
        attn = _moba_prompt(q0, q1, k, v, batch, seq)
        shifted = None
        conv_new = u.reshape(batch, seq, cd)[:, -(CONV_WIDTH - 1):]
    x_new, h2, scores = _mix(x, attn, gb, u, shifted, mod, conv_w, g_attn, g_conv, w_o, g2,
                             w_query, sub_keys, tile, tps)
    tabs = _route(scores)
    x_out = _peer(h2, u_tab, vt_tab, tabs, x_new, mod, g_final, tile, tps, final)
    return x_out, k, v, conv_new


def kernel(x_prompt, x_sample, cache_k, cache_v, state_conv, page_table, c_prompt, c_sample,
           w_ada, b_ada, g_norm1, g_norm2, w_in, conv_w, g_out_attn, g_out_conv, w_o,
           w_query, sub_keys, u_experts, v_experts, g_final):
    b, s, d = x_prompt.shape
    bd, sd, _ = x_sample.shape
    depth = w_ada.shape[0]
    assert s % MOBA_BLOCK == 0 and (bd * sd) % LANES == 0

    mod = _ada_mod(jnp.concatenate([c_prompt, c_sample], axis=0), w_ada, b_ada)

    scale = HEAD_DIM ** -0.5
    cols0 = jnp.concatenate([jnp.arange(HEAD_DIM) + (hk * Q_GROUP) * HEAD_DIM for hk in range(KV_HEADS)])
    cols1 = cols0 + HEAD_DIM

    xp = x_prompt.reshape(b * s, d)
    xs = x_sample.reshape(bd * sd, d)
    zero_conv = jnp.zeros((b, CONV_WIDTH - 1, conv_w.shape[-1]), x_prompt.dtype)
    ks_p, vs_p, cs_p, ks_s, vs_s, cs_s = [], [], [], [], [], []
    for l in range(depth):
        wl = w_in[l]
        w_in_l = jnp.concatenate([wl[:, cols0] * scale, wl[:, cols1] * scale, wl[:, ATTN_DIM:]],
                                 axis=1).astype(BF16)
        weights = (g_norm1[l][None], w_in_l, conv_w[l], g_out_attn[l][None], g_out_conv[l][None],
                   w_o[l].astype(BF16), g_norm2[l][None], _split_bf16(w_query[l]), sub_keys[l],
                   u_experts[l].astype(BF16), v_experts[l].T.astype(BF16), g_final[None])
        final = l == depth - 1
        mod_p = mod[l, :b][:, None, :]
        mod_s = jnp.repeat(mod[l, b:], sd, axis=0)
        xp, k, v, cn = _layer(xp, mod_p, None, zero_conv, weights, b, s, l, final)
        ks_p.append(k.reshape(b, s, KV_HEADS, HEAD_DIM))
        vs_p.append(v.reshape(b, s, KV_HEADS, HEAD_DIM))
        cs_p.append(cn)
        xs, k, v, cn = _layer(xs, mod_s, (cache_k, cache_v, page_table), state_conv[l], weights,
                              bd, sd, l, final)
        ks_s.append(k.reshape(bd, sd, KV_HEADS, HEAD_DIM))
        vs_s.append(v.reshape(bd, sd, KV_HEADS, HEAD_DIM))
        cs_s.append(cn)
    return (xp.reshape(b, s, d), xs.reshape(bd, sd, d),
            jnp.stack(ks_p), jnp.stack(vs_p), jnp.stack(cs_p),
            jnp.stack(ks_s), jnp.stack(vs_s), jnp.stack(cs_s))
```

```python
import functools

import jax
import jax.numpy as jnp
from jax import lax
from jax.experimental import pallas as pl
from jax.experimental.pallas import tpu as pltpu

F32 = jnp.float32
BF16 = jnp.bfloat16

N_HEADS = 8
KV_HEADS = 4
HEAD_DIM = 64
Q_GROUP = N_HEADS // KV_HEADS
ATTN_DIM = N_HEADS * HEAD_DIM
KV_DIM = KV_HEADS * HEAD_DIM
MOBA_BLOCK = 256
MOBA_TOPK = 3
Q_CHUNK = 128
CONV_WIDTH = 3
PEER_HEADS = 8
PEER_KEYS = 128
PEER_HALF = 64
PEER_TOPK = 16
NORM_EPS = 1e-6
NEG_BIG = -1e30
POS_BIG = 1e30

VMEM_LIMIT_BYTES = 56 * 1024 * 1024
LANES = 128
BF16_ROWS = 16
PEER_EXPERT_TILE = 2048

NT_DIMS = (((1,), (1,)), ((), ()))


def _cparams(sem):
    return pltpu.CompilerParams(dimension_semantics=sem, vmem_limit_bytes=VMEM_LIMIT_BYTES)


def _rms(x, g):
    y = x * lax.rsqrt(jnp.mean(x * x, axis=-1, keepdims=True) + NORM_EPS)
    return y * g


def _alibi_slope(h):
    return 2.0 ** (-8.0 * (h + 1) / N_HEADS)


def _ada_kernel(c_ref, w_ref, b_ref, o_ref):
    c = c_ref[...]
    o_ref[...] = jnp.dot(c * jax.nn.sigmoid(c), w_ref[...], precision=lax.Precision.HIGHEST,
                         preferred_element_type=F32) + b_ref[...]


def _ada_mod(c_all, w_ada, b_ada):
    depth, d, d6 = w_ada.shape
    n = c_all.shape[0]
    return pl.pallas_call(
        _ada_kernel,
        grid=(depth, d6 // d),
        in_specs=[pl.BlockSpec((n, d), lambda l, j: (0, 0)),
                  pl.BlockSpec((None, d, d), lambda l, j: (l, 0, j)),
                  pl.BlockSpec((None, 1, d), lambda l, j: (l, 0, j))],
        out_specs=pl.BlockSpec((None, n, d), lambda l, j: (l, 0, j)),
        out_shape=jax.ShapeDtypeStruct((depth, n, d6), F32),
        compiler_params=_cparams(("parallel", "parallel")),
        name="ada_mod",
    )(c_all, w_ada, b_ada.reshape(depth, 1, d6))


def _mod_spec(mod, col, tile, tiles_per_seq):
    d = mod.shape[-1] // 6
    if mod.ndim == 3:
        return pl.BlockSpec((None, 1, d), lambda i, *_: (i // tiles_per_seq, 0, col))
    return pl.BlockSpec((tile, d), lambda i, *_: (i, col))


def _proj_kernel(x_ref, sh_ref, sc_ref, g_ref, w_ref, q0_ref, q1_ref, k_ref, v_ref, gb_ref, u_ref):
    h = _rms(x_ref[...], g_ref[...]) * (1.0 + sc_ref[...]) + sh_ref[...]
    p = jnp.dot(h.astype(BF16), w_ref[...], preferred_element_type=F32)
    q0_ref[...] = p[:, 0:KV_DIM]
    q1_ref[...] = p[:, KV_DIM:2 * KV_DIM]
    k_ref[...] = p[:, 2 * KV_DIM:3 * KV_DIM]
    v_ref[...] = p[:, 3 * KV_DIM:4 * KV_DIM]
    c0 = 4 * KV_DIM
    cd = (p.shape[1] - c0) // 3
    gb_ref[...] = p[:, c0:c0 + cd]
    u_ref[...] = p[:, c0 + cd:c0 + 2 * cd] * p[:, c0 + 2 * cd:c0 + 3 * cd]


def _proj_in(x, mod, g1, w_in, tile, tiles_per_seq):
    t, d = x.shape
    n_in = w_in.shape[1]
    cd = (n_in - 4 * KV_DIM) // 3
    row = lambda i: (i, 0)
    outs = [jax.ShapeDtypeStruct((t, KV_DIM), F32)] * 4 + [jax.ShapeDtypeStruct((t, cd), F32)] * 2
    out_specs = [pl.BlockSpec((tile, KV_DIM), row)] * 4 + [pl.BlockSpec((tile, cd), row)] * 2
    return pl.pallas_call(
        _proj_kernel,
        grid=(t // tile,),
        in_specs=[pl.BlockSpec((tile, d), row),
                  _mod_spec(mod, 0, tile, tiles_per_seq),
                  _mod_spec(mod, 1, tile, tiles_per_seq),
                  pl.BlockSpec((1, d), lambda i: (0, 0)),
                  pl.BlockSpec((d, n_in), lambda i: (0, 0))],
        out_specs=out_specs,
        out_shape=outs,
        compiler_params=_cparams(("parallel",)),
        name="proj_in",
    )(x, mod, mod, g1, w_in)


def _moba_prompt_kernel(q0_ref, q1_ref, k_ref, v_ref, o_ref, kb_s, vb_s, km4_s, e_s, alibi_s):
    c = pl.program_id(1)
    s_len = k_ref.shape[0]
    nb = s_len // MOBA_BLOCK

    @pl.when(c == 0)
    def _():
        kf = k_ref[...]
        kb_s[...] = kf.astype(BF16)
        vb_s[...] = v_ref[...].astype(BF16)
        km = jnp.sum(kf.reshape(nb, MOBA_BLOCK, KV_DIM), axis=1) * (1.0 / MOBA_BLOCK)
        lane_head = lax.broadcasted_iota(jnp.int32, km.shape, 1) // HEAD_DIM
        parts = [jnp.where(lane_head == hk, km, 0.0) for hk in range(KV_HEADS)]
        parts.append(jnp.zeros((LANES - KV_HEADS * nb, KV_DIM), F32))
        km4_s[...] = jnp.concatenate(parts, axis=0).T
        key_blk = lax.broadcasted_iota(jnp.int32, (LANES, s_len), 1) // MOBA_BLOCK
        row = lax.broadcasted_iota(jnp.int32, (LANES, s_len), 0)
        e_s[...] = ((key_blk == (row & (nb - 1))) & (row < KV_HEADS * nb)).astype(BF16)
        tk = lax.broadcasted_iota(jnp.int32, (1, s_len), 1).astype(F32)
        for h in range(N_HEADS):
            alibi_s[h:h + 1, :] = _alibi_slope(h) * tk

    chunks_per_blk = MOBA_BLOCK // Q_CHUNK
    own = c // chunks_per_blk
    for n_past in range(nb):
        @pl.when(own == n_past)
        def _(n_past=n_past):
            _moba_prompt_chunk(n_past, (c % chunks_per_blk) * Q_CHUNK, q0_ref, q1_ref, o_ref,
                               kb_s, vb_s, km4_s, e_s, alibi_s)


def _block_penalty(q, km4, n_past, nb):
    gate = jnp.dot(q, km4, precision=lax.Precision.HIGHEST, preferred_element_type=F32)
    n_of = lax.broadcasted_iota(jnp.int32, gate.shape, 1) & (nb - 1)
    gate = jnp.where(n_of < n_past, gate, -jnp.inf)
    rank = jnp.zeros(gate.shape, F32)
    for delta in range(1, n_past):
        above = pltpu.roll(gate, LANES - delta, 1)
        rank = rank + jnp.where(above > gate, jnp.where(n_of + delta < nb, 1.0, 0.0), 0.0)
        below = pltpu.roll(gate, delta, 1)
        rank = rank + jnp.where(below >= gate, jnp.where(n_of >= delta, 1.0, 0.0), 0.0)
    return jnp.where(rank < float(MOBA_TOPK), 0.0, NEG_BIG).astype(BF16)


def _moba_prompt_chunk(n_past, q_off, q0_ref, q1_ref, o_ref, kb_s, vb_s, km4_s, e_s, alibi_s):
    nb = kb_s.shape[0] // MOBA_BLOCK
    n_keys = (n_past + 1) * MOBA_BLOCK
    past_keys = n_past * MOBA_BLOCK
    q_pos = q_off + lax.broadcasted_iota(jnp.int32, (Q_CHUNK, MOBA_BLOCK), 0)
    causal = jnp.where(q_pos >= lax.broadcasted_iota(jnp.int32, (Q_CHUNK, MOBA_BLOCK), 1), 0.0, NEG_BIG)
    causal = jnp.concatenate([causal] * KV_HEADS, axis=0)
    lane_head = lax.broadcasted_iota(jnp.int32, (Q_CHUNK, KV_DIM), 1) // HEAD_DIM
    need_gate = n_past > MOBA_TOPK
    if need_gate:
        km4 = km4_s[...]
        lane_grp = lax.broadcasted_iota(jnp.int32, (Q_CHUNK, LANES), 1) // nb
    kb = kb_s[0:n_keys, :]
    vb = vb_s[0:n_keys, :]
    outs = [None] * N_HEADS
    for g, q_ref in enumerate((q0_ref, q1_ref)):
        q = q_ref[...]
        qpad = jnp.concatenate([jnp.where(lane_head == hk, q, 0.0) for hk in range(KV_HEADS)], axis=0)
        s = lax.dot_general(qpad.astype(BF16), kb, NT_DIMS, preferred_element_type=F32)
        st = s + jnp.concatenate(
            [jnp.broadcast_to(alibi_s[hk * Q_GROUP + g:hk * Q_GROUP + g + 1, 0:n_keys], (Q_CHUNK, n_keys))
             for hk in range(KV_HEADS)], axis=0)
        if need_gate:
            pen = _block_penalty(q, km4, n_past, nb)
            pen = jnp.concatenate([jnp.where(lane_grp == hk, pen, jnp.zeros((), BF16))
                                   for hk in range(KV_HEADS)], axis=0)
            bias = jnp.dot(pen, e_s[:, 0:past_keys], preferred_element_type=F32)
            st = jnp.concatenate([st[:, :past_keys] + bias, st[:, past_keys:] + causal], axis=1)
        elif n_past > 0:
            st = jnp.concatenate([st[:, :past_keys], st[:, past_keys:] + causal], axis=1)
        else:
            st = st + causal
        mx = jnp.max(st, axis=1, keepdims=True)
        p = jnp.exp(st - mx)
        l = jnp.sum(p, axis=1, keepdims=True)
        o = jnp.dot(p.astype(BF16), vb, preferred_element_type=F32) / l
        for hk in range(KV_HEADS):
            outs[hk * Q_GROUP + g] = o[hk * Q_CHUNK:(hk + 1) * Q_CHUNK, hk * HEAD_DIM:(hk + 1) * HEAD_DIM]
    o_ref[...] = jnp.concatenate(outs, axis=1)


def _moba_prompt(q0, q1, k, v, batch, seq):
    n_chunks = seq // Q_CHUNK
    nb = seq // MOBA_BLOCK
    assert nb & (nb - 1) == 0 and KV_HEADS * nb <= LANES
    qspec = pl.BlockSpec((Q_CHUNK, KV_DIM), lambda b, c: (b * n_chunks + c, 0))
    kvspec = pl.BlockSpec((seq, KV_DIM), lambda b, c: (b, 0))
    return pl.pallas_call(
        _moba_prompt_kernel,
        grid=(batch, n_chunks),
        in_specs=[qspec, qspec, kvspec, kvspec],
        out_specs=pl.BlockSpec((Q_CHUNK, ATTN_DIM), lambda b, c: (b * n_chunks + c, 0)),
        out_shape=jax.ShapeDtypeStruct((batch * seq, ATTN_DIM), F32),
        scratch_shapes=[pltpu.VMEM((seq, KV_DIM), BF16), pltpu.VMEM((seq, KV_DIM), BF16),
                        pltpu.VMEM((KV_DIM, LANES), F32), pltpu.VMEM((LANES, seq), BF16),
                        pltpu.VMEM((N_HEADS, seq), F32)],
        compiler_params=_cparams(("parallel", "arbitrary")),
        name="moba_prompt",
    )(q0, q1, k, v)


def _moba_decode_kernel(pt_ref, qs_ref, kn_ref, vn_ref, *refs, pps, n_blk, pos0, page):
    del pt_ref
    kp = refs[:pps]
    vp = refs[pps:2 * pps]
    o_ref = refs[2 * pps]
    m_s, l_s, km_s, o_s = refs[2 * pps + 1:]
    step = pl.program_id(1)
    n_rows = KV_HEADS * qs_ref.shape[0]
    sd = qs_ref.shape[0] // Q_GROUP
    ppb = MOBA_BLOCK // page
    bps = pps // ppb

    qs = qs_ref[...]
    lane_head = lax.broadcasted_iota(jnp.int32, qs.shape, 1) // HEAD_DIM
    qpad = jnp.concatenate([jnp.where(lane_head == hk, qs, 0.0) for hk in range(KV_HEADS)], axis=0)
    qpad_b = qpad.astype(BF16)
    row = lax.broadcasted_iota(jnp.int32, (n_rows, 1), 0)
    head = (row // qs.shape[0]) * Q_GROUP + (row % qs.shape[0]) // sd
    qi = row % sd
    slope = jnp.zeros((n_rows, 1), F32)
    for h in range(N_HEADS):
        slope = jnp.where(head == h, _alibi_slope(h), slope)
    lane = lax.broadcasted_iota(jnp.int32, (n_rows, LANES), 1)

    @pl.when(step == 0)
    def _():
        m_s[...] = jnp.zeros_like(m_s)
        l_s[...] = jnp.zeros_like(l_s)
        km_s[...] = jnp.zeros_like(km_s)

    kcol = lax.broadcasted_iota(jnp.int32, (KV_DIM, LANES), 1)
    for n in range(bps):
        blk = step * bps + n
        sts = []
        ksum = jnp.zeros((KV_DIM, 1), F32)
        for i in range(ppb):
            kt = kp[n * ppb + i][...]
            s = jnp.dot(qpad_b, kt.astype(BF16), preferred_element_type=F32)
            tk = blk * MOBA_BLOCK + i * page + lax.broadcasted_iota(jnp.int32, (1, page), 1)
            sts.append(s - slope * ((pos0 + qi) - tk).astype(F32))
            ksum = ksum + jnp.sum(kt, axis=1, keepdims=True)
        mx = sts[0].max(axis=1, keepdims=True)
        for st in sts[1:]:
            mx = jnp.maximum(mx, st.max(axis=1, keepdims=True))
        l = jnp.zeros_like(mx)
        o = jnp.zeros((n_rows, KV_DIM), F32)
        for i, st in enumerate(sts):
            p = jnp.exp(st - mx)
            l = l + jnp.sum(p, axis=1, keepdims=True)
            o = o + lax.dot_general(p.astype(BF16), vp[n * ppb + i][...].astype(BF16), NT_DIMS,
                                    preferred_element_type=F32)
        o_s[blk] = o
        here = lane == blk
        m_s[...] = jnp.where(here, mx, m_s[...])
        l_s[...] = jnp.where(here, l, l_s[...])
        km_s[...] = jnp.where(kcol == blk, ksum * (1.0 / MOBA_BLOCK), km_s[...])

    @pl.when(step == pl.num_programs(1) - 1)
    def _():
        valid = lane < n_blk
        gates = jnp.dot(qpad, km_s[...], precision=lax.Precision.HIGHEST, preferred_element_type=F32)
        work = jnp.where(valid, gates, -jnp.inf)
        sel = jnp.zeros((n_rows, LANES), F32)
        for _ in range(min(MOBA_TOPK, n_blk)):
            top = jnp.max(work, axis=1, keepdims=True)
            first = jnp.min(jnp.where(work == top, lane, LANES), axis=1, keepdims=True)
            hit = (lane == first) & valid
            sel = jnp.where(hit, 1.0, sel)
            work = jnp.where(hit, -jnp.inf, work)
        picked = sel > 0.0
        own_scores = []
        for j in range(sd):
            sj = jnp.sum(qpad * kn_ref[j:j + 1, :], axis=1, keepdims=True)
            dj = (qi - j).astype(F32)
            own_scores.append(jnp.where(dj >= 0.0, sj - slope * dj, NEG_BIG))
        m_tot = jnp.max(jnp.where(picked, m_s[...], NEG_BIG), axis=1, keepdims=True)
        for sj in own_scores:
            m_tot = jnp.maximum(m_tot, sj)
        wn = jnp.where(picked, jnp.exp(m_s[...] - m_tot), 0.0)
        denom = jnp.sum(wn * l_s[...], axis=1, keepdims=True)
        acc = jnp.zeros((n_rows, KV_DIM), F32)
        for j, sj in enumerate(own_scores):
            pj = jnp.exp(sj - m_tot)
            denom = denom + pj
            acc = acc + pj * vn_ref[j:j + 1, :]
        for n in range(n_blk):
            coef = jnp.sum(jnp.where(lane == n, wn, 0.0), axis=1, keepdims=True)
            acc = acc + coef * o_s[n]
        o_ref[...] = acc / denom


def _moba_decode(qs, k_new, v_new, cache_k, cache_v, page_table, layer):
    bd, rows8, _ = qs.shape
    sd = k_new.shape[1]
    n_pages = page_table.shape[1]
    page = cache_k.shape[2]
    n_pool = cache_k.shape[1]
    ppb = MOBA_BLOCK // page
    n_blk = n_pages // ppb
    assert n_pages % ppb == 0 and n_blk <= LANES
    pps = min(16, n_pages)
    assert n_pages % pps == 0 and pps % ppb == 0
    ck = cache_k.transpose(0, 1, 3, 4, 2).reshape(cache_k.shape[0], n_pool, KV_DIM, page)
    cv = cache_v.transpose(0, 1, 3, 4, 2).reshape(cache_v.shape[0], n_pool, KV_DIM, page)
    n_rows = KV_HEADS * rows8

    def page_spec(i):
        return pl.BlockSpec((None, None, KV_DIM, page),
                            lambda b, s, pt: (layer, pt[b, s * pps + i], 0, 0))

    seq_spec = lambda r: pl.BlockSpec((None, r, KV_DIM), lambda b, s, pt: (b, 0, 0))
    grid_spec = pltpu.PrefetchScalarGridSpec(
        num_scalar_prefetch=1,
        grid=(bd, n_pages // pps),
        in_specs=[seq_spec(rows8), seq_spec(sd), seq_spec(sd)]
        + [page_spec(i) for i in range(pps)] * 2,
        out_specs=seq_spec(n_rows),
        scratch_shapes=[pltpu.VMEM((n_rows, LANES), F32)] * 2
        + [pltpu.VMEM((KV_DIM, LANES), F32), pltpu.VMEM((n_blk, n_rows, KV_DIM), F32)],
    )
    kern = functools.partial(_moba_decode_kernel, pps=pps, n_blk=n_blk,
                             pos0=n_pages * page, page=page)
    return pl.pallas_call(
        kern,
        grid_spec=grid_spec,
        out_shape=jax.ShapeDtypeStruct((bd, n_rows, KV_DIM), F32),
        compiler_params=_cparams(("parallel", "arbitrary")),
        name="moba_decode",
    )(page_table, qs, k_new, v_new, *([ck] * pps), *([cv] * pps))


def _mix_kernel(*refs, decode, tiles_per_seq):
    if decode:
        (x_ref, attn_ref, gb_ref, u_ref, u1_ref, u2_ref, ga1_ref, sh2_ref, sc2_ref, cw_ref, ga_ref,
         gc_ref, wo_ref, g2_ref, wq_ref, sk_ref, xo_ref, h2_ref, st_ref) = refs
        u0 = u_ref[...]
        u1 = u1_ref[...]
        u2 = u2_ref[...]
    else:
        (x_ref, attn_ref, gb_ref, u_ref, halo_ref, ga1_ref, sh2_ref, sc2_ref, cw_ref, ga_ref,
         gc_ref, wo_ref, g2_ref, wq_ref, sk_ref, xo_ref, h2_ref, st_ref) = refs
        u0 = u_ref[...]
        first = pl.program_id(0) % tiles_per_seq == 0
        halo = jnp.where(first, 0.0, halo_ref[...])
        hr = halo.shape[0]
        rowi = lax.broadcasted_iota(jnp.int32, u0.shape, 0)
        u1 = jnp.where(rowi < 1, halo[hr - 1:hr, :], pltpu.roll(u0, 1, 0))
        u2 = jnp.where(rowi < 1, halo[hr - 2:hr - 1, :],
                       jnp.where(rowi < 2, halo[hr - 1:hr, :], pltpu.roll(u0, 2, 0)))
    cw = cw_ref[...]
    y = cw[0:1, :] * u2
    y = y + cw[1:2, :] * u1
    y = y + cw[2:3, :] * u0
    yc = gb_ref[...] * y
    cat = jnp.concatenate([_rms(attn_ref[...], ga_ref[...]), _rms(yc, gc_ref[...])], axis=1)
    mixed = jnp.dot(cat.astype(BF16), wo_ref[...], preferred_element_type=F32)
    x_new = x_ref[...] + ga1_ref[...] * mixed
    xo_ref[...] = x_new
    h2 = _rms(x_new, g2_ref[...]) * (1.0 + sc2_ref[...]) + sh2_ref[...]
    h_hi = h2.astype(BF16)
    h2_ref[...] = h_hi
    h_lo = (h2 - h_hi.astype(F32)).astype(BF16)
    q = jnp.dot(h_hi, wq_ref[0], preferred_element_type=F32) + (
        jnp.dot(h_hi, wq_ref[1], preferred_element_type=F32)
        + jnp.dot(h_lo, wq_ref[0], preferred_element_type=F32))
    qt = q.T
    for h in range(PEER_HEADS):
        for p in range(2):
            r0 = (h * 2 + p) * PEER_HALF
            st_ref[h, p] = jnp.dot(sk_ref[p], qt[r0:r0 + PEER_HALF, :],
                                   precision=lax.Precision.HIGHEST, preferred_element_type=F32)


def _mix(x, attn, gb, u, shifted, mod, conv_w, g_attn, g_conv, w_o, g2, w_query, sub_keys,
         tile, tiles_per_seq):
    t, d = x.shape
    cd = u.shape[1]
    decode = shifted is not None
    row = lambda i: (i, 0)
    full2 = lambda a: pl.BlockSpec(a.shape, lambda i: (0,) * a.ndim)
    if decode:
        conv_in = [u, shifted[0], shifted[1]]
        conv_specs = [pl.BlockSpec((tile, cd), row)] * 3
    else:
        halo_rows = 8
        conv_in = [u, u]
        conv_specs = [pl.BlockSpec((tile, cd), row),
                      pl.BlockSpec((halo_rows, cd),
                                   lambda i: (jnp.maximum(i * (tile // halo_rows) - 1, 0), 0))]
    kern = functools.partial(_mix_kernel, decode=decode, tiles_per_seq=tiles_per_seq)
    return pl.pallas_call(
        kern,
        grid=(t // tile,),
        in_specs=[pl.BlockSpec((tile, d), row), pl.BlockSpec((tile, attn.shape[1]), row),
                  pl.BlockSpec((tile, cd), row)] + conv_specs
        + [_mod_spec(mod, 2, tile, tiles_per_seq), _mod_spec(mod, 3, tile, tiles_per_seq),
           _mod_spec(mod, 4, tile, tiles_per_seq),
           full2(conv_w), full2(g_attn), full2(g_conv), full2(w_o), full2(g2), full2(w_query),
           full2(sub_keys)],
        out_specs=[pl.BlockSpec((tile, d), row), pl.BlockSpec((tile, d), row),
                   pl.BlockSpec((PEER_HEADS, 2, PEER_KEYS, tile), lambda i: (0, 0, 0, i))],
        out_shape=[jax.ShapeDtypeStruct((t, d), F32), jax.ShapeDtypeStruct((t, d), BF16),
                   jax.ShapeDtypeStruct((PEER_HEADS, 2, PEER_KEYS, t), F32)],
        compiler_params=_cparams(("parallel",)),
        name="mix_scores",
    )(x, attn, gb, *conv_in, mod, mod, mod, conv_w, g_attn, g_conv, w_o, g2, w_query, sub_keys)


def _top_rows(work, n_take, vals_ref):
    n = work.shape[0]
    idx = lax.broadcasted_iota(jnp.int32, work.shape, 0).astype(F32)
    rank = jnp.full(work.shape, float(n_take), F32)
    for r in range(n_take):
        top = jnp.max(work, axis=0, keepdims=True)
        first = jnp.min(jnp.where(work == top, idx, float(n)), axis=0, keepdims=True)
        hit = idx == first
        rank = jnp.where(hit, float(r), rank)
        work = jnp.where(hit, -jnp.inf, work)
        if vals_ref is not None:
            vals_ref[r:r + 1, :] = top
    return rank


def _top_rows_distinct(work, n_take, vals_ref):
    rank = jnp.full(work.shape, float(n_take), F32)
    for r in range(n_take):
        top = jnp.max(work, axis=0, keepdims=True)
        hit = work == top
        rank = jnp.where(hit, float(r), rank)
        work = jnp.where(hit, -jnp.inf, work)
        if vals_ref is not None:
            vals_ref[r:r + 1, :] = top
    n_marked = jnp.sum(jnp.where(rank < float(n_take), 1.0, 0.0), axis=0, keepdims=True)
    return rank, n_marked


def _dup_bf16(x):
    u = lax.bitcast_convert_type(x, jnp.uint32)
    r = (u + jnp.uint32(0x7FFF) + ((u >> 16) & jnp.uint32(1))) >> 16
    return r | (r << 16)


def _write_tables(h, refs, s1, s2, rank1, rank2, c1, z, a0, b0):
    c1_ref, e1_ref, r2_ref, e2_ref = refs
    c1_ref[h] = _dup_bf16(c1)
    e1_ref[h] = _dup_bf16(jnp.exp(s1 - a0) / z)
    r2_ref[h] = pltpu.bitcast(rank2.astype(BF16), jnp.uint32)
    e2_ref[h] = pltpu.bitcast(jnp.exp(s2 - b0).astype(BF16), jnp.uint32)


def _route_head_exact(h, s_ref, refs, a_s, b_s):
    s1 = s_ref[h, 0]
    s2 = s_ref[h, 1]
    rank1 = _top_rows(s1, PEER_TOPK, a_s)
    rank2 = _top_rows(s2, PEER_TOPK, b_s)
    a = a_s[...]
    b = b_s[...]
    cand = jnp.concatenate([a[r:r + 1, :] + b for r in range(PEER_TOPK)], axis=0)
    crank = _top_rows(cand, PEER_TOPK, None)
    sel = crank < float(PEER_TOPK)
    top0 = a[0:1, :] + b[0:1, :]
    z = jnp.sum(jnp.where(sel, jnp.exp(cand - top0), 0.0), axis=0, keepdims=True)
    self = jnp.where(sel, 1.0, 0.0)
    c1 = jnp.zeros_like(s1)
    for r in range(PEER_TOPK):
        cnt = jnp.sum(self[r * PEER_TOPK:(r + 1) * PEER_TOPK, :], axis=0, keepdims=True)
        c1 = jnp.where(rank1 == float(r), cnt, c1)
    _write_tables(h, refs, s1, s2, rank1, rank2, c1, z, a[0:1, :], b[0:1, :])


def _route_head_distinct(h, s_ref, refs, a_s, b_s):
    s1 = s_ref[h, 0]
    s2 = s_ref[h, 1]
    rank1, n1 = _top_rows_distinct(s1, PEER_TOPK, a_s)
    rank2, n2 = _top_rows_distinct(s2, PEER_TOPK, b_s)
    a = a_s[...]
    b = b_s[...]
    half = PEER_TOPK // 2
    cand = jnp.concatenate([a[0:1, :] + b] + [a[r:r + 1, :] + b[0:half, :] for r in range(1, half)]
                           + [a[half:, :] + b[0:1, :]], axis=0)
    row = lax.broadcasted_iota(jnp.int32, cand.shape, 0)
    assert half == 8
    r_mid = ((row - PEER_TOPK) >> 3) + 1
    c_mid = (row - PEER_TOPK) & 7
    in_mid = (row >= PEER_TOPK) & (row < PEER_TOPK + half * (half - 1))
    cand = jnp.where(in_mid & ((r_mid + 1) * (c_mid + 1) > PEER_TOPK), -jnp.inf, cand)
    crank, nc = _top_rows_distinct(cand, PEER_TOPK, None)
    sel = crank < float(PEER_TOPK)
    top0 = a[0:1, :] + b[0:1, :]
    z = jnp.sum(jnp.where(sel, jnp.exp(cand - top0), 0.0), axis=0, keepdims=True)
    self = jnp.where(sel, 1.0, 0.0)
    cnts = [jnp.sum(self[0:PEER_TOPK, :], axis=0, keepdims=True)]
    for r in range(1, half):
        lo = PEER_TOPK + (r - 1) * half
        cnts.append(jnp.sum(self[lo:lo + half, :], axis=0, keepdims=True))
    tail = PEER_TOPK + half * (half - 1)
    cnts += [self[tail + r:tail + r + 1, :] for r in range(half)]
    c1 = jnp.zeros_like(s1)
    for r in range(PEER_TOPK):
        c1 = jnp.where(rank1 == float(r), cnts[r], c1)
    _write_tables(h, refs, s1, s2, rank1, rank2, c1, z, a[0:1, :], b[0:1, :])
    want = float(PEER_TOPK)
    return jnp.abs(n1 - want) + jnp.abs(n2 - want) + jnp.abs(nc - want)


def _route_kernel(s_ref, c1_ref, e1_ref, r2_ref, e2_ref, a_s, b_s):
    refs = (c1_ref, e1_ref, r2_ref, e2_ref)

    def head_pair(i, carry):
        heads = (2 * i, 2 * i + 1)
        ties = [_route_head_distinct(h, s_ref, refs, a_s.at[k], b_s.at[k]) for k, h in enumerate(heads)]
        for k, h in enumerate(heads):
            @pl.when(jnp.max(ties[k]) > 0.0)
            def _(k=k, h=h):
                _route_head_exact(h, s_ref, refs, a_s.at[k], b_s.at[k])

        return carry

    lax.fori_loop(0, PEER_HEADS // 2, head_pair, 0)


def _route(scores):
    _, _, nk, t = scores.shape
    tile = LANES
    wspec = pl.BlockSpec((PEER_HEADS, nk, tile), lambda i: (0, 0, i))
    pspec = pl.BlockSpec((PEER_HEADS, nk // 2, tile), lambda i: (0, 0, i))
    words = jax.ShapeDtypeStruct((PEER_HEADS, nk, t), jnp.uint32)
    packed = jax.ShapeDtypeStruct((PEER_HEADS, nk // 2, t), jnp.uint32)
    return pl.pallas_call(
        _route_kernel,
        grid=(t // tile,),
        in_specs=[pl.BlockSpec((PEER_HEADS, 2, nk, tile), lambda i: (0, 0, 0, i))],
        out_specs=[wspec, wspec, pspec, pspec],
        out_shape=[words, words, packed, packed],
        scratch_shapes=[pltpu.VMEM((2, PEER_TOPK, tile), F32)] * 2,
        compiler_params=_cparams(("parallel",)),
        name="peer_route",
    )(scores)


def _gelu(x):
    return 0.5 * x * (1.0 + lax.erf(x * (2.0 ** -0.5)))


def _peer_kernel(h2_ref, u_ref, vt_ref, c1_ref, e1_ref, r2_ref, e2_ref, x_ref, ga2_ref, gf_ref,
                 o_ref, acc_s, w_a, w_b, ht_a, ht_b, *, final):
    j = pl.program_id(1)
    n_e = pl.num_programs(1) - 1
    slot = j % 2

    @pl.when(j == 0)
    def _():
        acc_s[...] = jnp.zeros_like(acc_s)

    def expert_tile(w_prev, w_cur, first=False, last=False):
        xb = h2_ref[...]
        tt = xb.shape[0]
        grp = PEER_KEYS // BF16_ROWS
        pair = 2 * PEER_KEYS
        n_stages = u_ref.shape[0] // pair
        v_rows = vt_ref.shape[0] // n_stages
        hts = (ht_a, ht_b)

        def scores(k):
            rows = slice(k * pair, (k + 1) * pair)
            hts[k % 2][...] = lax.dot_general(u_ref[rows, :], xb, NT_DIMS, preferred_element_type=F32)

        def second_product(k):
            rows = slice(k * v_rows, (k + 1) * v_rows)
            acc_s[rows, :] += jnp.dot(vt_ref[rows, :], w_prev[...], preferred_element_type=F32)

        def activations(k):
            ht_ref = hts[k % 2]
            c1_rows = [[pltpu.bitcast(jnp.broadcast_to(c1_ref[h, 2 * k + sub:2 * k + sub + 1, :], (8, tt)), BF16)
                        for h in range(PEER_HEADS)] for sub in range(2)]
            e1_rows = [[pltpu.bitcast(jnp.broadcast_to(e1_ref[h, 2 * k + sub:2 * k + sub + 1, :], (8, tt)), BF16)
                        for h in range(PEER_HEADS)] for sub in range(2)]
            for cb in range(tt // LANES):
                cols = slice(cb * LANES, (cb + 1) * LANES)
                gates = [jnp.zeros((PEER_KEYS, LANES), BF16) for _ in range(2)]
                for h in range(PEER_HEADS):
                    r2 = pltpu.bitcast(r2_ref[h, :, cols], BF16)
                    e2 = pltpu.bitcast(e2_ref[h, :, cols], BF16)
                    for sub in range(2):
                        c1 = jnp.concatenate([c1_rows[sub][h][:, cols]] * grp, axis=0)
                        e1 = jnp.concatenate([e1_rows[sub][h][:, cols]] * grp, axis=0)
                        gates[sub] = gates[sub] + jnp.where(r2 < c1, e2, jnp.zeros((), BF16)) * e1
                for sub in range(2):
                    r0 = k * pair + sub * PEER_KEYS
                    act = _gelu(ht_ref[sub * PEER_KEYS:(sub + 1) * PEER_KEYS, cols]).astype(BF16)
                    w_cur[r0:r0 + PEER_KEYS, cols] = act * gates[sub]

        if last:
            for k in range(n_stages):
                second_product(k)
            return
        scores(0)
        for k in range(1, n_stages):
            activations(k - 1)
            scores(k)
            if not first:
                second_product(k - 1)
        activations(n_stages - 1)
        if not first:
            second_product(n_stages - 1)

    @pl.when(j == 0)
    def _():
        expert_tile(None, w_a, first=True)

    middle = (j > 0) & (j < n_e)

    @pl.when(middle & (slot == 0))
    def _():
        expert_tile(w_b, w_a)

    @pl.when(middle & (slot == 1))
    def _():
        expert_tile(w_a, w_b)

    @pl.when((j == n_e) & (slot == 0))
    def _():
        expert_tile(w_b, None, last=True)

    @pl.when((j == n_e) & (slot == 1))
    def _():
        expert_tile(w_a, None, last=True)

    @pl.when(j == n_e)
    def _():
        out = x_ref[...] + ga2_ref[...] * acc_s[...].T
        if final:
            out = _rms(out, gf_ref[...])
        o_ref[...] = out


def _peer(h2, u_tab, vt_tab, tabs, x_new, mod, g_final, tile, tiles_per_seq, final):
    t, d = x_new.shape
    n_exp = u_tab.shape[0]
    e_tile = min(PEER_EXPERT_TILE, n_exp)
    n_stages = e_tile // (2 * PEER_KEYS)
    assert n_exp % e_tile == 0 and e_tile % (2 * PEER_KEYS) == 0 and d % (n_stages * BF16_ROWS) == 0
    n_e = n_exp // e_tile
    c1, e1, r2, e2 = tabs
    nk = c1.shape[1]
    row = lambda i, j: (i, 0)
    cur = lambda j: jnp.minimum(j, n_e - 1)
    i1_spec = pl.BlockSpec((PEER_HEADS, e_tile // nk, tile), lambda i, j: (0, cur(j), i))
    i2_spec = pl.BlockSpec((PEER_HEADS, nk // 2, tile), lambda i, j: (0, 0, i))
    return pl.pallas_call(
        functools.partial(_peer_kernel, final=final),
        grid=(t // tile, n_e + 1),
        in_specs=[pl.BlockSpec((tile, d), row),
                  pl.BlockSpec((e_tile, d), lambda i, j: (cur(j), 0)),
                  pl.BlockSpec((d, e_tile), lambda i, j: (0, jnp.maximum(j - 1, 0))),
                  i1_spec, i1_spec, i2_spec, i2_spec,
                  pl.BlockSpec((tile, d), row),
                  _mod_spec(mod, 5, tile, tiles_per_seq),
                  pl.BlockSpec((1, d), lambda i, j: (0, 0))],
        out_specs=pl.BlockSpec((tile, d), row),
        out_shape=jax.ShapeDtypeStruct((t, d), F32),
        scratch_shapes=[pltpu.VMEM((d, tile), F32),
                        pltpu.VMEM((e_tile, tile), BF16), pltpu.VMEM((e_tile, tile), BF16),
                        pltpu.VMEM((2 * PEER_KEYS, tile), F32), pltpu.VMEM((2 * PEER_KEYS, tile), F32)],
        compiler_params=_cparams(("parallel", "arbitrary")),
        name="peer_dense",
    )(h2, u_tab, vt_tab, c1, e1, r2, e2, x_new, mod, g_final)


def _split_bf16(w):
    hi = w.astype(BF16)
    return jnp.stack([hi, (w - hi.astype(F32)).astype(BF16)])


def _token_tile(seq, want):
    tile = min(want, seq)
    assert seq % tile == 0
    return tile


def _layer(x, mod, past, conv_prev, weights, batch, seq, layer, final):
    (g1, w_in, conv_w, g_attn, g_conv, w_o, g2, w_query, sub_keys, u_tab, vt_tab, g_final) = weights
    decode = past is not None
    t, d = x.shape
    if decode:
        tile, tps = t, 1
    else:
        tile = _token_tile(seq, 512)
        tps = seq // tile
    q0, q1, k, v, gb, u = _proj_in(x, mod, g1, w_in, tile, tps)
    cd = u.shape[1]
    if decode:
        cache_k, cache_v, page_table = past
        qs = jnp.concatenate([q0.reshape(batch, seq, KV_DIM), q1.reshape(batch, seq, KV_DIM)], axis=1)
        o = _moba_decode(qs, k.reshape(batch, seq, KV_DIM), v.reshape(batch, seq, KV_DIM),
                         cache_k, cache_v, page_table, layer)
        o = o.reshape(batch, KV_HEADS, Q_GROUP, seq, KV_HEADS, HEAD_DIM)
        o = jnp.stack([o[:, hk, :, :, hk, :] for hk in range(KV_HEADS)], axis=1)
        attn = o.transpose(0, 3, 1, 2, 4).reshape(t, ATTN_DIM)
        xp = jnp.concatenate([conv_prev, u.reshape(batch, seq, cd)], axis=1)
        shifted = (xp[:, 1:1 + seq].reshape(t, cd), xp[:, 0:seq].reshape(t, cd))
        conv_new = xp[:, -(CONV_WIDTH - 1):]
    else:
        attn = _moba_prompt(q0, q1, k, v, batch, seq)
        shifted = None
        conv_new = u.reshape(batch, seq, cd)[:, -(CONV_WIDTH - 1):]
    x_new, h2, scores = _mix(x, attn, gb, u, shifted, mod, conv_w, g_attn, g_conv, w_o, g2,
                             w_query, sub_keys, tile, tps)
    tabs = _route(scores)
    x_out = _peer(h2, u_tab, vt_tab, tabs, x_new, mod, g_final, tile, tps, final)
    return x_out, k, v, conv_new


def kernel(x_prompt, x_sample, cache_k, cache_v, state_conv, page_table, c_prompt, c_sample,
           w_ada, b_ada, g_norm1, g_norm2, w_in, conv_w, g_out_attn, g_out_conv, w_o,
           w_query, sub_keys, u_experts, v_experts, g_final):
    b, s, d = x_prompt.shape
    bd, sd, _ = x_sample.shape
    depth = w_ada.shape[0]
    assert s % MOBA_BLOCK == 0 and (bd * sd) % LANES == 0

    mod = _ada_mod(jnp.concatenate([c_prompt, c_sample], axis=0), w_ada, b_ada)

    scale = HEAD_DIM ** -0.5
    cols0 = jnp.concatenate([jnp.arange(HEAD_DIM) + (hk * Q_GROUP) * HEAD_DIM for hk in range(KV_HEADS)])
    cols1 = cols0 + HEAD_DIM

    xp = x_prompt.reshape(b * s, d)
    xs = x_sample.reshape(bd * sd, d)
    zero_conv = jnp.zeros((b, CONV_WIDTH - 1, conv_w.shape[-1]), x_prompt.dtype)
    ks_p, vs_p, cs_p, ks_s, vs_s, cs_s = [], [], [], [], [], []
    for l in range(depth):
        wl = w_in[l]
        w_in_l = jnp.concatenate([wl[:, cols0] * scale, wl[:, cols1] * scale, wl[:, ATTN_DIM:]],
                                 axis=1).astype(BF16)
        weights = (g_norm1[l][None], w_in_l, conv_w[l], g_out_attn[l][None], g_out_conv[l][None],
                   w_o[l].astype(BF16), g_norm2[l][None], _split_bf16(w_query[l]), sub_keys[l],
                   u_experts[l].astype(BF16), v_experts[l].T.astype(BF16), g_final[None])
        final = l == depth - 1
        mod_p = mod[l, :b][:, None, :]
        mod_s = jnp.repeat(mod[l, b:], sd, axis=0)
        xp, k, v, cn = _layer(xp, mod_p, None, zero_conv, weights, b, s, l, final)
        ks_p.append(k.reshape(b, s, KV_HEADS, HEAD_DIM))
        vs_p.append(v.reshape(b, s, KV_HEADS, HEAD_DIM))
        cs_p.append(cn)
        xs, k, v, cn = _layer(xs, mod_s, (cache_k, cache_v, page_table), state_conv[l], weights,
                              bd, sd, l, final)
        ks_s.append(k.reshape(bd, sd, KV_HEADS, HEAD_DIM))
        vs_s.append(v.reshape(bd, sd, KV_HEADS, HEAD_DIM))
        cs_s.append(cn)
    return (xp.reshape(b, s, d), xs.reshape(bd, sd, d),
            jnp.stack(ks_p), jnp.stack(vs_p), jnp.stack(cs_p),
            jnp.stack(ks_s), jnp.stack(vs_s), jnp.stack(cs_s))
```

```python
import functools

import jax
import jax.numpy as jnp
from jax import lax
from jax.experimental import pallas as pl
from jax.experimental.pallas import tpu as pltpu

F32 = jnp.float32
BF16 = jnp.bfloat16

N_HEADS = 8
KV_HEADS = 4
HEAD_DIM = 64
Q_GROUP = N_HEADS // KV_HEADS
ATTN_DIM = N_HEADS * HEAD_DIM
KV_DIM = KV_HEADS * HEAD_DIM
MOBA_BLOCK = 256
MOBA_TOPK = 3
Q_CHUNK = 128
CONV_WIDTH = 3
PEER_HEADS = 8
PEER_KEYS = 128
PEER_HALF = 64
PEER_TOPK = 16
NORM_EPS = 1e-6
NEG_BIG = -1e30
POS_BIG = 1e30

VMEM_LIMIT_BYTES = 56 * 1024 * 1024
LANES = 128
BF16_ROWS = 16
PEER_EXPERT_TILE = 2048

NT_DIMS = (((1,), (1,)), ((), ()))


def _cparams(sem):
    return pltpu.CompilerParams(dimension_semantics=sem, vmem_limit_bytes=VMEM_LIMIT_BYTES)


def _rms(x, g):
    y = x * lax.rsqrt(jnp.mean(x * x, axis=-1, keepdims=True) + NORM_EPS)
    return y * g


def _alibi_slope(h):
    return 2.0 ** (-8.0 * (h + 1) / N_HEADS)


def _ada_kernel(c_ref, w_ref, b_ref, o_ref):
    c = c_ref[...]
    o_ref[...] = jnp.dot(c * jax.nn.sigmoid(c), w_ref[...], precision=lax.Precision.HIGHEST,
                         preferred_element_type=F32) + b_ref[...]


def _ada_mod(c_all, w_ada, b_ada):
    depth, d, d6 = w_ada.shape
    n = c_all.shape[0]
    return pl.pallas_call(
        _ada_kernel,
        grid=(depth, d6 // d),
        in_specs=[pl.BlockSpec((n, d), lambda l, j: (0, 0)),
                  pl.BlockSpec((None, d, d), lambda l, j: (l, 0, j)),
                  pl.BlockSpec((None, 1, d), lambda l, j: (l, 0, j))],
        out_specs=pl.BlockSpec((None, n, d), lambda l, j: (l, 0, j)),
        out_shape=jax.ShapeDtypeStruct((depth, n, d6), F32),
        compiler_params=_cparams(("parallel", "parallel")),
        name="ada_mod",
    )(c_all, w_ada, b_ada.reshape(depth, 1, d6))


def _mod_spec(mod, col, tile, tiles_per_seq):
    d = mod.shape[-1] // 6
    if mod.ndim == 3:
        return pl.BlockSpec((None, 1, d), lambda i, *_: (i // tiles_per_seq, 0, col))
    return pl.BlockSpec((tile, d), lambda i, *_: (i, col))


def _proj_kernel(x_ref, sh_ref, sc_ref, g_ref, w_ref, q0_ref, q1_ref, k_ref, v_ref, gb_ref, u_ref):
    h = _rms(x_ref[...], g_ref[...]) * (1.0 + sc_ref[...]) + sh_ref[...]
    p = jnp.dot(h.astype(BF16), w_ref[...], preferred_element_type=F32)
    q0_ref[...] = p[:, 0:KV_DIM]
    q1_ref[...] = p[:, KV_DIM:2 * KV_DIM]
    k_ref[...] = p[:, 2 * KV_DIM:3 * KV_DIM]
    v_ref[...] = p[:, 3 * KV_DIM:4 * KV_DIM]
    c0 = 4 * KV_DIM
    cd = (p.shape[1] - c0) // 3
    gb_ref[...] = p[:, c0:c0 + cd]
    u_ref[...] = p[:, c0 + cd:c0 + 2 * cd] * p[:, c0 + 2 * cd:c0 + 3 * cd]


def _proj_in(x, mod, g1, w_in, tile, tiles_per_seq):
    t, d = x.shape
    n_in = w_in.shape[1]
    cd = (n_in - 4 * KV_DIM) // 3
    row = lambda i: (i, 0)
    outs = [jax.ShapeDtypeStruct((t, KV_DIM), F32)] * 4 + [jax.ShapeDtypeStruct((t, cd), F32)] * 2
    out_specs = [pl.BlockSpec((tile, KV_DIM), row)] * 4 + [pl.BlockSpec((tile, cd), row)] * 2
    return pl.pallas_call(
        _proj_kernel,
        grid=(t // tile,),
        in_specs=[pl.BlockSpec((tile, d), row),
                  _mod_spec(mod, 0, tile, tiles_per_seq),
                  _mod_spec(mod, 1, tile, tiles_per_seq),
                  pl.BlockSpec((1, d), lambda i: (0, 0)),
                  pl.BlockSpec((d, n_in), lambda i: (0, 0))],
        out_specs=out_specs,
        out_shape=outs,
        compiler_params=_cparams(("parallel",)),
        name="proj_in",
    )(x, mod, mod, g1, w_in)


def _moba_prompt_kernel(q0_ref, q1_ref, k_ref, v_ref, o_ref, kb_s, vb_s, km4_s, e_s, alibi_s):
    c = pl.program_id(1)
    s_len = k_ref.shape[0]
    nb = s_len // MOBA_BLOCK

    @pl.when(c == 0)
    def _():
        kf = k_ref[...]
        kb_s[...] = kf.astype(BF16)
        vb_s[...] = v_ref[...].astype(BF16)
        km = jnp.sum(kf.reshape(nb, MOBA_BLOCK, KV_DIM), axis=1) * (1.0 / MOBA_BLOCK)
        lane_head = lax.broadcasted_iota(jnp.int32, km.shape, 1) // HEAD_DIM
        parts = [jnp.where(lane_head == hk, km, 0.0) for hk in range(KV_HEADS)]
        parts.append(jnp.zeros((LANES - KV_HEADS * nb, KV_DIM), F32))
        km4_s[...] = jnp.concatenate(parts, axis=0).T
        key_blk = lax.broadcasted_iota(jnp.int32, (LANES, s_len), 1) // MOBA_BLOCK
        row = lax.broadcasted_iota(jnp.int32, (LANES, s_len), 0)
        e_s[...] = ((key_blk == (row & (nb - 1))) & (row < KV_HEADS * nb)).astype(BF16)
        tk = lax.broadcasted_iota(jnp.int32, (1, s_len), 1).astype(F32)
        for h in range(N_HEADS):
            alibi_s[h:h + 1, :] = _alibi_slope(h) * tk

    chunks_per_blk = MOBA_BLOCK // Q_CHUNK
    own = c // chunks_per_blk
    for n_past in range(nb):
        @pl.when(own == n_past)
        def _(n_past=n_past):
            _moba_prompt_chunk(n_past, (c % chunks_per_blk) * Q_CHUNK, q0_ref, q1_ref, o_ref,
                               kb_s, vb_s, km4_s, e_s, alibi_s)


def _block_penalty(q, km4, n_past, nb):
    gate = jnp.dot(q, km4, precision=lax.Precision.HIGHEST, preferred_element_type=F32)
    n_of = lax.broadcasted_iota(jnp.int32, gate.shape, 1) & (nb - 1)
    gate = jnp.where(n_of < n_past, gate, -jnp.inf)
    rank = jnp.zeros(gate.shape, F32)
    for delta in range(1, n_past):
        above = pltpu.roll(gate, LANES - delta, 1)
        rank = rank + jnp.where(above > gate, jnp.where(n_of + delta < nb, 1.0, 0.0), 0.0)
        below = pltpu.roll(gate, delta, 1)
        rank = rank + jnp.where(below >= gate, jnp.where(n_of >= delta, 1.0, 0.0), 0.0)
    return jnp.where(rank < float(MOBA_TOPK), 0.0, NEG_BIG).astype(BF16)


def _moba_prompt_chunk(n_past, q_off, q0_ref, q1_ref, o_ref, kb_s, vb_s, km4_s, e_s, alibi_s):
    nb = kb_s.shape[0] // MOBA_BLOCK
    n_keys = (n_past + 1) * MOBA_BLOCK
    past_keys = n_past * MOBA_BLOCK
    q_pos = q_off + lax.broadcasted_iota(jnp.int32, (Q_CHUNK, MOBA_BLOCK), 0)
    causal = jnp.where(q_pos >= lax.broadcasted_iota(jnp.int32, (Q_CHUNK, MOBA_BLOCK), 1), 0.0, NEG_BIG)
    causal = jnp.concatenate([causal] * KV_HEADS, axis=0)
    lane_head = lax.broadcasted_iota(jnp.int32, (Q_CHUNK, KV_DIM), 1) // HEAD_DIM
    need_gate = n_past > MOBA_TOPK
    if need_gate:
        km4 = km4_s[...]
        lane_grp = lax.broadcasted_iota(jnp.int32, (Q_CHUNK, LANES), 1) // nb
    kb = kb_s[0:n_keys, :]
    vb = vb_s[0:n_keys, :]
    outs = [None] * N_HEADS
    for g, q_ref in enumerate((q0_ref, q1_ref)):
        q = q_ref[...]
        qpad = jnp.concatenate([jnp.where(lane_head == hk, q, 0.0) for hk in range(KV_HEADS)], axis=0)
        s = lax.dot_general(qpad.astype(BF16), kb, NT_DIMS, preferred_element_type=F32)
        st = s + jnp.concatenate(
            [jnp.broadcast_to(alibi_s[hk * Q_GROUP + g:hk * Q_GROUP + g + 1, 0:n_keys], (Q_CHUNK, n_keys))
             for hk in range(KV_HEADS)], axis=0)
        if need_gate:
            pen = _block_penalty(q, km4, n_past, nb)
            pen = jnp.concatenate([jnp.where(lane_grp == hk, pen, jnp.zeros((), BF16))
                                   for hk in range(KV_HEADS)], axis=0)
            bias = jnp.dot(pen, e_s[:, 0:past_keys], preferred_element_type=F32)
            st = jnp.concatenate([st[:, :past_keys] + bias, st[:, past_keys:] + causal], axis=1)
        elif n_past > 0:
            st = jnp.concatenate([st[:, :past_keys], st[:, past_keys:] + causal], axis=1)
        else:
            st = st + causal
        mx = jnp.max(st, axis=1, keepdims=True)
        p = jnp.exp(st - mx)
        l = jnp.sum(p, axis=1, keepdims=True)
        o = jnp.dot(p.astype(BF16), vb, preferred_element_type=F32) / l
        for hk in range(KV_HEADS):
            outs[hk * Q_GROUP + g] = o[hk * Q_CHUNK:(hk + 1) * Q_CHUNK, hk * HEAD_DIM:(hk + 1) * HEAD_DIM]
    o_ref[...] = jnp.concatenate(outs, axis=1)


def _moba_prompt(q0, q1, k, v, batch, seq):
    n_chunks = seq // Q_CHUNK
    nb = seq // MOBA_BLOCK
    assert nb & (nb - 1) == 0 and KV_HEADS * nb <= LANES
    qspec = pl.BlockSpec((Q_CHUNK, KV_DIM), lambda b, c: (b * n_chunks + c, 0))
    kvspec = pl.BlockSpec((seq, KV_DIM), lambda b, c: (b, 0))
    return pl.pallas_call(
        _moba_prompt_kernel,
        grid=(batch, n_chunks),
        in_specs=[qspec, qspec, kvspec, kvspec],
        out_specs=pl.BlockSpec((Q_CHUNK, ATTN_DIM), lambda b, c: (b * n_chunks + c, 0)),
        out_shape=jax.ShapeDtypeStruct((batch * seq, ATTN_DIM), F32),
        scratch_shapes=[pltpu.VMEM((seq, KV_DIM), BF16), pltpu.VMEM((seq, KV_DIM), BF16),
                        pltpu.VMEM((KV_DIM, LANES), F32), pltpu.VMEM((LANES, seq), BF16),
                        pltpu.VMEM((N_HEADS, seq), F32)],
        compiler_params=_cparams(("parallel", "arbitrary")),
        name="moba_prompt",
    )(q0, q1, k, v)


def _moba_decode_kernel(pt_ref, qs_ref, kn_ref, vn_ref, *refs, pps, n_blk, pos0, page):
    del pt_ref
    kp = refs[:pps]
    vp = refs[pps:2 * pps]
    o_ref = refs[2 * pps]
    m_s, l_s, km_s, o_s = refs[2 * pps + 1:]
    step = pl.program_id(1)
    n_rows = KV_HEADS * qs_ref.shape[0]
    sd = qs_ref.shape[0] // Q_GROUP
    ppb = MOBA_BLOCK // page
    bps = pps // ppb

    qs = qs_ref[...]
    lane_head = lax.broadcasted_iota(jnp.int32, qs.shape, 1) // HEAD_DIM
    qpad = jnp.concatenate([jnp.where(lane_head == hk, qs, 0.0) for hk in range(KV_HEADS)], axis=0)
    qpad_b = qpad.astype(BF16)
    row = lax.broadcasted_iota(jnp.int32, (n_rows, 1), 0)
    head = (row // qs.shape[0]) * Q_GROUP + (row % qs.shape[0]) // sd
    qi = row % sd
    slope = jnp.zeros((n_rows, 1), F32)
    for h in range(N_HEADS):
        slope = jnp.where(head == h, _alibi_slope(h), slope)
    lane = lax.broadcasted_iota(jnp.int32, (n_rows, LANES), 1)

    @pl.when(step == 0)
    def _():
        m_s[...] = jnp.zeros_like(m_s)
        l_s[...] = jnp.zeros_like(l_s)
        km_s[...] = jnp.zeros_like(km_s)

    kcol = lax.broadcasted_iota(jnp.int32, (KV_DIM, LANES), 1)
    for n in range(bps):
        blk = step * bps + n
        sts = []
        ksum = jnp.zeros((KV_DIM, 1), F32)
        for i in range(ppb):
            kt = kp[n * ppb + i][...]
            s = jnp.dot(qpad_b, kt.astype(BF16), preferred_element_type=F32)
            tk = blk * MOBA_BLOCK + i * page + lax.broadcasted_iota(jnp.int32, (1, page), 1)
            sts.append(s - slope * ((pos0 + qi) - tk).astype(F32))
            ksum = ksum + jnp.sum(kt, axis=1, keepdims=True)
        mx = sts[0].max(axis=1, keepdims=True)
        for st in sts[1:]:
            mx = jnp.maximum(mx, st.max(axis=1, keepdims=True))
        l = jnp.zeros_like(mx)
        o = jnp.zeros((n_rows, KV_DIM), F32)
        for i, st in enumerate(sts):
            p = jnp.exp(st - mx)
            l = l + jnp.sum(p, axis=1, keepdims=True)
            o = o + lax.dot_general(p.astype(BF16), vp[n * ppb + i][...].astype(BF16), NT_DIMS,
                                    preferred_element_type=F32)
        o_s[blk] = o
        here = lane == blk
        m_s[...] = jnp.where(here, mx, m_s[...])
        l_s[...] = jnp.where(here, l, l_s[...])
        km_s[...] = jnp.where(kcol == blk, ksum * (1.0 / MOBA_BLOCK), km_s[...])

    @pl.when(step == pl.num_programs(1) - 1)
    def _():
        valid = lane < n_blk
        gates = jnp.dot(qpad, km_s[...], precision=lax.Precision.HIGHEST, preferred_element_type=F32)
        work = jnp.where(valid, gates, -jnp.inf)
        sel = jnp.zeros((n_rows, LANES), F32)
        for _ in range(min(MOBA_TOPK, n_blk)):
            top = jnp.max(work, axis=1, keepdims=True)
            first = jnp.min(jnp.where(work == top, lane, LANES), axis=1, keepdims=True)
            hit = (lane == first) & valid
            sel = jnp.where(hit, 1.0, sel)
            work = jnp.where(hit, -jnp.inf, work)
        picked = sel > 0.0
        own_scores = []
        for j in range(sd):
            sj = jnp.sum(qpad * kn_ref[j:j + 1, :], axis=1, keepdims=True)
            dj = (qi - j).astype(F32)
            own_scores.append(jnp.where(dj >= 0.0, sj - slope * dj, NEG_BIG))
        m_tot = jnp.max(jnp.where(picked, m_s[...], NEG_BIG), axis=1, keepdims=True)
        for sj in own_scores:
            m_tot = jnp.maximum(m_tot, sj)
        wn = jnp.where(picked, jnp.exp(m_s[...] - m_tot), 0.0)
        denom = jnp.sum(wn * l_s[...], axis=1, keepdims=True)
        acc = jnp.zeros((n_rows, KV_DIM), F32)
        for j, sj in enumerate(own_scores):
            pj = jnp.exp(sj - m_tot)
            denom = denom + pj
            acc = acc + pj * vn_ref[j:j + 1, :]
        for n in range(n_blk):
            coef = jnp.sum(jnp.where(lane == n, wn, 0.0), axis=1, keepdims=True)
            acc = acc + coef * o_s[n]
        o_ref[...] = acc / denom


def _moba_decode(qs, k_new, v_new, cache_k, cache_v, page_table, layer):
    bd, rows8, _ = qs.shape
    sd = k_new.shape[1]
    n_pages = page_table.shape[1]
    page = cache_k.shape[2]
    n_pool = cache_k.shape[1]
    ppb = MOBA_BLOCK // page
    n_blk = n_pages // ppb
    assert n_pages % ppb == 0 and n_blk <= LANES
    pps = min(16, n_pages)
    assert n_pages % pps == 0 and pps % ppb == 0
    ck = cache_k.transpose(0, 1, 3, 4, 2).reshape(cache_k.shape[0], n_pool, KV_DIM, page)
    cv = cache_v.transpose(0, 1, 3, 4, 2).reshape(cache_v.shape[0], n_pool, KV_DIM, page)
    n_rows = KV_HEADS * rows8

    def page_spec(i):
        return pl.BlockSpec((None, None, KV_DIM, page),
                            lambda b, s, pt: (layer, pt[b, s * pps + i], 0, 0))

    seq_spec = lambda r: pl.BlockSpec((None, r, KV_DIM), lambda b, s, pt: (b, 0, 0))
    grid_spec = pltpu.PrefetchScalarGridSpec(
        num_scalar_prefetch=1,
        grid=(bd, n_pages // pps),
        in_specs=[seq_spec(rows8), seq_spec(sd), seq_spec(sd)]
        + [page_spec(i) for i in range(pps)] * 2,
        out_specs=seq_spec(n_rows),
        scratch_shapes=[pltpu.VMEM((n_rows, LANES), F32)] * 2
        + [pltpu.VMEM((KV_DIM, LANES), F32), pltpu.VMEM((n_blk, n_rows, KV_DIM), F32)],
    )
    kern = functools.partial(_moba_decode_kernel, pps=pps, n_blk=n_blk,
                             pos0=n_pages * page, page=page)
    return pl.pallas_call(
        kern,
        grid_spec=grid_spec,
        out_shape=jax.ShapeDtypeStruct((bd, n_rows, KV_DIM), F32),
        compiler_params=_cparams(("parallel", "arbitrary")),
        name="moba_decode",
    )(page_table, qs, k_new, v_new, *([ck] * pps), *([cv] * pps))


def _mix_kernel(*refs, decode, tiles_per_seq):
    if decode:
        (x_ref, attn_ref, gb_ref, u_ref, u1_ref, u2_ref, ga1_ref, sh2_ref, sc2_ref, cw_ref, ga_ref,
         gc_ref, wo_ref, g2_ref, wq_ref, sk_ref, xo_ref, h2_ref, st_ref) = refs
        u0 = u_ref[...]
        u1 = u1_ref[...]
        u2 = u2_ref[...]
    else:
        (x_ref, attn_ref, gb_ref, u_ref, halo_ref, ga1_ref, sh2_ref, sc2_ref, cw_ref, ga_ref,
         gc_ref, wo_ref, g2_ref, wq_ref, sk_ref, xo_ref, h2_ref, st_ref) = refs
        u0 = u_ref[...]
        first = pl.program_id(0) % tiles_per_seq == 0
        halo = jnp.where(first, 0.0, halo_ref[...])
        hr = halo.shape[0]
        rowi = lax.broadcasted_iota(jnp.int32, u0.shape, 0)
        u1 = jnp.where(rowi < 1, halo[hr - 1:hr, :], pltpu.roll(u0, 1, 0))
        u2 = jnp.where(rowi < 1, halo[hr - 2:hr - 1, :],
                       jnp.where(rowi < 2, halo[hr - 1:hr, :], pltpu.roll(u0, 2, 0)))
    cw = cw_ref[...]
    y = cw[0:1, :] * u2
    y = y + cw[1:2, :] * u1
    y = y + cw[2:3, :] * u0
    yc = gb_ref[...] * y
    cat = jnp.concatenate([_rms(attn_ref[...], ga_ref[...]), _rms(yc, gc_ref[...])], axis=1)
    mixed = jnp.dot(cat.astype(BF16), wo_ref[...], preferred_element_type=F32)
    x_new = x_ref[...] + ga1_ref[...] * mixed
    xo_ref[...] = x_new
    h2 = _rms(x_new, g2_ref[...]) * (1.0 + sc2_ref[...]) + sh2_ref[...]
    h_hi = h2.astype(BF16)
    h2_ref[...] = h_hi
    h_lo = (h2 - h_hi.astype(F32)).astype(BF16)
    q = jnp.dot(h_hi, wq_ref[0], preferred_element_type=F32) + (
        jnp.dot(h_hi, wq_ref[1], preferred_element_type=F32)
        + jnp.dot(h_lo, wq_ref[0], preferred_element_type=F32))
    qt = q.T
    qt_hi = qt.astype(BF16)
    qt_lo = (qt - qt_hi.astype(F32)).astype(BF16)
    for h in range(PEER_HEADS):
        for p in range(2):
            rows = slice((h * 2 + p) * PEER_HALF, (h * 2 + p + 1) * PEER_HALF)
            st_ref[h, p] = jnp.dot(sk_ref[0, p], qt_hi[rows, :], preferred_element_type=F32) + (
                jnp.dot(sk_ref[0, p], qt_lo[rows, :], preferred_element_type=F32)
                + jnp.dot(sk_ref[1, p], qt_hi[rows, :], preferred_element_type=F32))


def _mix(x, attn, gb, u, shifted, mod, conv_w, g_attn, g_conv, w_o, g2, w_query, sub_keys,
         tile, tiles_per_seq):
    t, d = x.shape
    cd = u.shape[1]
    decode = shifted is not None
    row = lambda i: (i, 0)
    full2 = lambda a: pl.BlockSpec(a.shape, lambda i: (0,) * a.ndim)
    if decode:
        conv_in = [u, shifted[0], shifted[1]]
        conv_specs = [pl.BlockSpec((tile, cd), row)] * 3
    else:
        halo_rows = 8
        conv_in = [u, u]
        conv_specs = [pl.BlockSpec((tile, cd), row),
                      pl.BlockSpec((halo_rows, cd),
                                   lambda i: (jnp.maximum(i * (tile // halo_rows) - 1, 0), 0))]
    kern = functools.partial(_mix_kernel, decode=decode, tiles_per_seq=tiles_per_seq)
    return pl.pallas_call(
        kern,
        grid=(t // tile,),
        in_specs=[pl.BlockSpec((tile, d), row), pl.BlockSpec((tile, attn.shape[1]), row),
                  pl.BlockSpec((tile, cd), row)] + conv_specs
        + [_mod_spec(mod, 2, tile, tiles_per_seq), _mod_spec(mod, 3, tile, tiles_per_seq),
           _mod_spec(mod, 4, tile, tiles_per_seq),
           full2(conv_w), full2(g_attn), full2(g_conv), full2(w_o), full2(g2), full2(w_query),
           full2(sub_keys)],
        out_specs=[pl.BlockSpec((tile, d), row), pl.BlockSpec((tile, d), row),
                   pl.BlockSpec((PEER_HEADS, 2, PEER_KEYS, tile), lambda i: (0, 0, 0, i))],
        out_shape=[jax.ShapeDtypeStruct((t, d), F32), jax.ShapeDtypeStruct((t, d), BF16),
                   jax.ShapeDtypeStruct((PEER_HEADS, 2, PEER_KEYS, t), F32)],
        compiler_params=_cparams(("parallel",)),
        name="mix_scores",
    )(x, attn, gb, *conv_in, mod, mod, mod, conv_w, g_attn, g_conv, w_o, g2, w_query, sub_keys)


def _top_rows(work, n_take, vals_ref):
    n = work.shape[0]
    idx = lax.broadcasted_iota(jnp.int32, work.shape, 0).astype(F32)
    rank = jnp.full(work.shape, float(n_take), F32)
    for r in range(n_take):
        top = jnp.max(work, axis=0, keepdims=True)
        first = jnp.min(jnp.where(work == top, idx, float(n)), axis=0, keepdims=True)
        hit = idx == first
        rank = jnp.where(hit, float(r), rank)
        work = jnp.where(hit, -jnp.inf, work)
        if vals_ref is not None:
            vals_ref[r:r + 1, :] = top
    return rank


def _top_rows_distinct(work, n_take, vals_ref, want_rank=True):
    rank = jnp.full(work.shape, float(n_take), F32) if want_rank else None
    for r in range(n_take):
        top = jnp.max(work, axis=0, keepdims=True)
        hit = work == top
        if want_rank:
            rank = jnp.where(hit, float(r), rank)
        work = jnp.where(hit, -jnp.inf, work)
        if vals_ref is not None:
            vals_ref[r:r + 1, :] = top
    n_marked = jnp.sum(jnp.where(work == -jnp.inf, 1.0, 0.0), axis=0, keepdims=True)
    return rank, n_marked


def _dup_bf16(x):
    u = lax.bitcast_convert_type(x, jnp.uint32)
    r = (u + jnp.uint32(0x7FFF) + ((u >> 16) & jnp.uint32(1))) >> 16
    return r | (r << 16)


def _write_tables(h, refs, s1, s2, rank2, c1, z, a0, b0):
    c1_ref, e1_ref, r2_ref, e2_ref = refs
    c1_ref[h] = _dup_bf16(c1)
    e1_ref[h] = _dup_bf16(jnp.exp(s1 - a0) / z)
    r2_ref[h] = pltpu.bitcast(rank2.astype(BF16), jnp.uint32)
    e2_ref[h] = pltpu.bitcast(jnp.exp(s2 - b0).astype(BF16), jnp.uint32)


def _route_head_exact(h, s_ref, refs, a_s, b_s):
    s1 = s_ref[h, 0]
    s2 = s_ref[h, 1]
    rank1 = _top_rows(s1, PEER_TOPK, a_s)
    rank2 = _top_rows(s2, PEER_TOPK, b_s)
    a = a_s[...]
    b = b_s[...]
    cand = jnp.concatenate([a[r:r + 1, :] + b for r in range(PEER_TOPK)], axis=0)
    crank = _top_rows(cand, PEER_TOPK, None)
    sel = crank < float(PEER_TOPK)
    top0 = a[0:1, :] + b[0:1, :]
    z = jnp.sum(jnp.where(sel, jnp.exp(cand - top0), 0.0), axis=0, keepdims=True)
    self = jnp.where(sel, 1.0, 0.0)
    c1 = jnp.zeros_like(s1)
    for r in range(PEER_TOPK):
        cnt = jnp.sum(self[r * PEER_TOPK:(r + 1) * PEER_TOPK, :], axis=0, keepdims=True)
        c1 = jnp.where(rank1 == float(r), cnt, c1)
    _write_tables(h, refs, s1, s2, rank2, c1, z, a[0:1, :], b[0:1, :])


def _route_head_distinct(h, s_ref, refs, a_s, b_s):
    s1 = s_ref[h, 0]
    s2 = s_ref[h, 1]
    _, n1 = _top_rows_distinct(s1, PEER_TOPK, a_s, want_rank=False)
    rank2, n2 = _top_rows_distinct(s2, PEER_TOPK, b_s)
    a = a_s[...]
    b = b_s[...]
    half = PEER_TOPK // 2
    cand = jnp.concatenate([a[0:1, :] + b] + [a[r:r + 1, :] + b[0:half, :] for r in range(1, half)]
                           + [a[half:, :] + b[0:1, :]], axis=0)
    row = lax.broadcasted_iota(jnp.int32, cand.shape, 0)
    assert half == 8
    r_mid = ((row - PEER_TOPK) >> 3) + 1
    c_mid = (row - PEER_TOPK) & 7
    in_mid = (row >= PEER_TOPK) & (row < PEER_TOPK + half * (half - 1))
    cand = jnp.where(in_mid & ((r_mid + 1) * (c_mid + 1) > PEER_TOPK), -jnp.inf, cand)
    crank, nc = _top_rows_distinct(cand, PEER_TOPK, None)
    sel = crank < float(PEER_TOPK)
    top0 = a[0:1, :] + b[0:1, :]
    z = jnp.sum(jnp.where(sel, jnp.exp(cand - top0), 0.0), axis=0, keepdims=True)
    self = jnp.where(sel, 1.0, 0.0)
    cnts = [jnp.sum(self[0:PEER_TOPK, :], axis=0, keepdims=True)]
    for r in range(1, half):
        lo = PEER_TOPK + (r - 1) * half
        cnts.append(jnp.sum(self[lo:lo + half, :], axis=0, keepdims=True))
    tail = PEER_TOPK + half * (half - 1)
    cnts += [self[tail + r:tail + r + 1, :] for r in range(half)]
    c1 = jnp.zeros_like(s1)
    for r in range(PEER_TOPK):
        c1 = jnp.where(s1 == a[r:r + 1, :], cnts[r], c1)
    _write_tables(h, refs, s1, s2, rank2, c1, z, a[0:1, :], b[0:1, :])
    want = float(PEER_TOPK)
    return jnp.abs(n1 - want) + jnp.abs(n2 - want) + jnp.abs(nc - want)


def _route_kernel(s_ref, c1_ref, e1_ref, r2_ref, e2_ref, a_s, b_s):
    refs = (c1_ref, e1_ref, r2_ref, e2_ref)

    def head_pair(i, carry):
        heads = (2 * i, 2 * i + 1)
        ties = [_route_head_distinct(h, s_ref, refs, a_s.at[k], b_s.at[k]) for k, h in enumerate(heads)]
        for k, h in enumerate(heads):
            @pl.when(jnp.max(ties[k]) > 0.0)
            def _(k=k, h=h):
                _route_head_exact(h, s_ref, refs, a_s.at[k], b_s.at[k])

        return carry

    lax.fori_loop(0, PEER_HEADS // 2, head_pair, 0)


def _route(scores):
    _, _, nk, t = scores.shape
    tile = LANES
    wspec = pl.BlockSpec((PEER_HEADS, nk, tile), lambda i: (0, 0, i))
    pspec = pl.BlockSpec((PEER_HEADS, nk // 2, tile), lambda i: (0, 0, i))
    words = jax.ShapeDtypeStruct((PEER_HEADS, nk, t), jnp.uint32)
    packed = jax.ShapeDtypeStruct((PEER_HEADS, nk // 2, t), jnp.uint32)
    return pl.pallas_call(
        _route_kernel,
        grid=(t // tile,),
        in_specs=[pl.BlockSpec((PEER_HEADS, 2, nk, tile), lambda i: (0, 0, 0, i))],
        out_specs=[wspec, wspec, pspec, pspec],
        out_shape=[words, words, packed, packed],
        scratch_shapes=[pltpu.VMEM((2, PEER_TOPK, tile), F32)] * 2,
        compiler_params=_cparams(("parallel",)),
        name="peer_route",
    )(scores)


def _gelu(x):
    return 0.5 * x * (1.0 + lax.erf(x * (2.0 ** -0.5)))


def _peer_kernel(h2_ref, u_ref, vt_ref, c1_ref, e1_ref, r2_ref, e2_ref, x_ref, ga2_ref, gf_ref,
                 o_ref, acc_s, w_a, w_b, ht_a, ht_b, *, final):
    j = pl.program_id(1)
    n_e = pl.num_programs(1) - 1
    slot = j % 2

    @pl.when(j == 0)
    def _():
        acc_s[...] = jnp.zeros_like(acc_s)

    def expert_tile(w_prev, w_cur, first=False, last=False):
        xb = h2_ref[...]
        tt = xb.shape[0]
        grp = PEER_KEYS // BF16_ROWS
        pair = 2 * PEER_KEYS
        n_stages = u_ref.shape[0] // pair
        v_rows = vt_ref.shape[0] // n_stages
        hts = (ht_a, ht_b)

        def scores(k):
            rows = slice(k * pair, (k + 1) * pair)
            hts[k % 2][...] = lax.dot_general(u_ref[rows, :], xb, NT_DIMS, preferred_element_type=F32)

        def second_product(k):
            rows = slice(k * v_rows, (k + 1) * v_rows)
            acc_s[rows, :] += jnp.dot(vt_ref[rows, :], w_prev[...], preferred_element_type=F32)

        def activations(k):
            ht_ref = hts[k % 2]
            c1_rows = [[pltpu.bitcast(jnp.broadcast_to(c1_ref[h, 2 * k + sub:2 * k + sub + 1, :], (8, tt)), BF16)
                        for h in range(PEER_HEADS)] for sub in range(2)]
            e1_rows = [[pltpu.bitcast(jnp.broadcast_to(e1_ref[h, 2 * k + sub:2 * k + sub + 1, :], (8, tt)), BF16)
                        for h in range(PEER_HEADS)] for sub in range(2)]
            for cb in range(tt // LANES):
                cols = slice(cb * LANES, (cb + 1) * LANES)
                gates = [jnp.zeros((PEER_KEYS, LANES), BF16) for _ in range(2)]
                for h in range(PEER_HEADS):
                    r2 = pltpu.bitcast(r2_ref[h, :, cols], BF16)
                    e2 = pltpu.bitcast(e2_ref[h, :, cols], BF16)
                    for sub in range(2):
                        c1 = jnp.concatenate([c1_rows[sub][h][:, cols]] * grp, axis=0)
                        e1 = jnp.concatenate([e1_rows[sub][h][:, cols]] * grp, axis=0)
                        gates[sub] = gates[sub] + jnp.where(r2 < c1, e2, jnp.zeros((), BF16)) * e1
                for sub in range(2):
                    r0 = k * pair + sub * PEER_KEYS
                    act = _gelu(ht_ref[sub * PEER_KEYS:(sub + 1) * PEER_KEYS, cols]).astype(BF16)
                    w_cur[r0:r0 + PEER_KEYS, cols] = act * gates[sub]

        if last:
            for k in range(n_stages):
                second_product(k)
            return
        scores(0)
        for k in range(1, n_stages):
            activations(k - 1)
            scores(k)
            if not first:
                second_product(k - 1)
        activations(n_stages - 1)
        if not first:
            second_product(n_stages - 1)

    @pl.when(j == 0)
    def _():
        expert_tile(None, w_a, first=True)

    middle = (j > 0) & (j < n_e)

    @pl.when(middle & (slot == 0))
    def _():
        expert_tile(w_b, w_a)

    @pl.when(middle & (slot == 1))
    def _():
        expert_tile(w_a, w_b)

    @pl.when((j == n_e) & (slot == 0))
    def _():
        expert_tile(w_b, None, last=True)

    @pl.when((j == n_e) & (slot == 1))
    def _():
        expert_tile(w_a, None, last=True)

    @pl.when(j == n_e)
    def _():
        out = x_ref[...] + ga2_ref[...] * acc_s[...].T
        if final:
            out = _rms(out, gf_ref[...])
        o_ref[...] = out


def _peer(h2, u_tab, vt_tab, tabs, x_new, mod, g_final, tile, tiles_per_seq, final):
    t, d = x_new.shape
    n_exp = u_tab.shape[0]
    e_tile = min(PEER_EXPERT_TILE, n_exp)
    n_stages = e_tile // (2 * PEER_KEYS)
    assert n_exp % e_tile == 0 and e_tile % (2 * PEER_KEYS) == 0 and d % (n_stages * BF16_ROWS) == 0
    n_e = n_exp // e_tile
    c1, e1, r2, e2 = tabs
    nk = c1.shape[1]
    row = lambda i, j: (i, 0)
    cur = lambda j: jnp.minimum(j, n_e - 1)
    i1_spec = pl.BlockSpec((PEER_HEADS, e_tile // nk, tile), lambda i, j: (0, cur(j), i))
    i2_spec = pl.BlockSpec((PEER_HEADS, nk // 2, tile), lambda i, j: (0, 0, i))
    return pl.pallas_call(
        functools.partial(_peer_kernel, final=final),
        grid=(t // tile, n_e + 1),
        in_specs=[pl.BlockSpec((tile, d), row),
                  pl.BlockSpec((e_tile, d), lambda i, j: (cur(j), 0)),
                  pl.BlockSpec((d, e_tile), lambda i, j: (0, jnp.maximum(j - 1, 0))),
                  i1_spec, i1_spec, i2_spec, i2_spec,
                  pl.BlockSpec((tile, d), row),
                  _mod_spec(mod, 5, tile, tiles_per_seq),
                  pl.BlockSpec((1, d), lambda i, j: (0, 0))],
        out_specs=pl.BlockSpec((tile, d), row),
        out_shape=jax.ShapeDtypeStruct((t, d), F32),
        scratch_shapes=[pltpu.VMEM((d, tile), F32),
                        pltpu.VMEM((e_tile, tile), BF16), pltpu.VMEM((e_tile, tile), BF16),
                        pltpu.VMEM((2 * PEER_KEYS, tile), F32), pltpu.VMEM((2 * PEER_KEYS, tile), F32)],
        compiler_params=_cparams(("parallel", "arbitrary")),
        name="peer_dense",
    )(h2, u_tab, vt_tab, c1, e1, r2, e2, x_new, mod, g_final)


def _split_bf16(w):
    hi = w.astype(BF16)
    return jnp.stack([hi, (w - hi.astype(F32)).astype(BF16)])


def _token_tile(seq, want):
    tile = min(want, seq)
    assert seq % tile == 0
    return tile


def _layer(x, mod, past, conv_prev, weights, batch, seq, layer, final):
    (g1, w_in, conv_w, g_attn, g_conv, w_o, g2, w_query, sub_keys, u_tab, vt_tab, g_final) = weights
    decode = past is not None
    t, d = x.shape
    if decode:
        tile, tps = t, 1
    else:
        tile = _token_tile(seq, 512)
        tps = seq // tile
    q0, q1, k, v, gb, u = _proj_in(x, mod, g1, w_in, tile, tps)
    cd = u.shape[1]
    if decode:
        cache_k, cache_v, page_table = past
        qs = jnp.concatenate([q0.reshape(batch, seq, KV_DIM), q1.reshape(batch, seq, KV_DIM)], axis=1)
        o = _moba_decode(qs, k.reshape(batch, seq, KV_DIM), v.reshape(batch, seq, KV_DIM),
                         cache_k, cache_v, page_table, layer)
        o = o.reshape(batch, KV_HEADS, Q_GROUP, seq, KV_HEADS, HEAD_DIM)
        o = jnp.stack([o[:, hk, :, :, hk, :] for hk in range(KV_HEADS)], axis=1)
        attn = o.transpose(0, 3, 1, 2, 4).reshape(t, ATTN_DIM)
        xp = jnp.concatenate([conv_prev, u.reshape(batch, seq, cd)], axis=1)
        shifted = (xp[:, 1:1 + seq].reshape(t, cd), xp[:, 0:seq].reshape(t, cd))
        conv_new = xp[:, -(CONV_WIDTH - 1):]
    else:
        attn = _moba_prompt(q0, q1, k, v, batch, seq)
        shifted = None
        conv_new = u.reshape(batch, seq, cd)[:, -(CONV_WIDTH - 1):]
    x_new, h2, scores = _mix(x, attn, gb, u, shifted, mod, conv_w, g_attn, g_conv, w_o, g2,
                             w_query, sub_keys, tile, tps)
    tabs = _route(scores)
    x_out = _peer(h2, u_tab, vt_tab, tabs, x_new, mod, g_final, tile, tps, final)
    return x_out, k, v, conv_new


def kernel(x_prompt, x_sample, cache_k, cache_v, state_conv, page_table, c_prompt, c_sample,
           w_ada, b_ada, g_norm1, g_norm2, w_in, conv_w, g_out_attn, g_out_conv, w_o,
           w_query, sub_keys, u_experts, v_experts, g_final):
    b, s, d = x_prompt.shape
    bd, sd, _ = x_sample.shape
    depth = w_ada.shape[0]
    assert s % MOBA_BLOCK == 0 and (bd * sd) % LANES == 0

    mod = _ada_mod(jnp.concatenate([c_prompt, c_sample], axis=0), w_ada, b_ada)

    scale = HEAD_DIM ** -0.5
    cols0 = jnp.concatenate([jnp.arange(HEAD_DIM) + (hk * Q_GROUP) * HEAD_DIM for hk in range(KV_HEADS)])
    cols1 = cols0 + HEAD_DIM

    xp = x_prompt.reshape(b * s, d)
    xs = x_sample.reshape(bd * sd, d)
    zero_conv = jnp.zeros((b, CONV_WIDTH - 1, conv_w.shape[-1]), x_prompt.dtype)
    ks_p, vs_p, cs_p, ks_s, vs_s, cs_s = [], [], [], [], [], []
    for l in range(depth):
        wl = w_in[l]
        w_in_l = jnp.concatenate([wl[:, cols0] * scale, wl[:, cols1] * scale, wl[:, ATTN_DIM:]],
                                 axis=1).astype(BF16)
        weights = (g_norm1[l][None], w_in_l, conv_w[l], g_out_attn[l][None], g_out_conv[l][None],
                   w_o[l].astype(BF16), g_norm2[l][None], _split_bf16(w_query[l]), _split_bf16(sub_keys[l]),
                   u_experts[l].astype(BF16), v_experts[l].T.astype(BF16), g_final[None])
        final = l == depth - 1
        mod_p = mod[l, :b][:, None, :]
        mod_s = jnp.repeat(mod[l, b:], sd, axis=0)
        xp, k, v, cn = _layer(xp, mod_p, None, zero_conv, weights, b, s, l, final)
        ks_p.append(k.reshape(b, s, KV_HEADS, HEAD_DIM))
        vs_p.append(v.reshape(b, s, KV_HEADS, HEAD_DIM))
        cs_p.append(cn)
        xs, k, v, cn = _layer(xs, mod_s, (cache_k, cache_v, page_table), state_conv[l], weights,
                              bd, sd, l, final)
        ks_s.append(k.reshape(bd, sd, KV_HEADS, HEAD_DIM))
        vs_s.append(v.reshape(bd, sd, KV_HEADS, HEAD_DIM))
        cs_s.append(cn)
    return (xp.reshape(b, s, d), xs.reshape(bd, sd, d),
            jnp.stack(ks_p), jnp.stack(vs_p), jnp.stack(cs_p),
            jnp.stack(ks_s), jnp.stack(vs_s), jnp.stack(cs_s))
```

```python
import functools

import jax
import jax.numpy as jnp
from jax import lax
from jax.experimental import pallas as pl
from jax.experimental.pallas import tpu as pltpu

F32 = jnp.float32
BF16 = jnp.bfloat16

N_HEADS = 8
KV_HEADS = 4
HEAD_DIM = 64
Q_GROUP = N_HEADS // KV_HEADS
ATTN_DIM = N_HEADS * HEAD_DIM
KV_DIM = KV_HEADS * HEAD_DIM
MOBA_BLOCK = 256
MOBA_TOPK = 3
Q_CHUNK = 128
CONV_WIDTH = 3
PEER_HEADS = 8
PEER_KEYS = 128
PEER_HALF = 64
PEER_TOPK = 16
NORM_EPS = 1e-6
NEG_BIG = -1e30
POS_BIG = 1e30

VMEM_LIMIT_BYTES = 56 * 1024 * 1024
LANES = 128
BF16_ROWS = 16
PEER_EXPERT_TILE = 2048

NT_DIMS = (((1,), (1,)), ((), ()))


def _cparams(sem):
    return pltpu.CompilerParams(dimension_semantics=sem, vmem_limit_bytes=VMEM_LIMIT_BYTES)


def _rms(x, g):
    y = x * lax.rsqrt(jnp.mean(x * x, axis=-1, keepdims=True) + NORM_EPS)
    return y * g


def _alibi_slope(h):
    return 2.0 ** (-8.0 * (h + 1) / N_HEADS)


def _ada_kernel(c_ref, w_ref, b_ref, o_ref):
    c = c_ref[...]
    o_ref[...] = jnp.dot(c * jax.nn.sigmoid(c), w_ref[...], precision=lax.Precision.HIGHEST,
                         preferred_element_type=F32) + b_ref[...]


def _ada_mod(c_all, w_ada, b_ada):
    depth, d, d6 = w_ada.shape
    n = c_all.shape[0]
    return pl.pallas_call(
        _ada_kernel,
        grid=(depth, d6 // d),
        in_specs=[pl.BlockSpec((n, d), lambda l, j: (0, 0)),
                  pl.BlockSpec((None, d, d), lambda l, j: (l, 0, j)),
                  pl.BlockSpec((None, 1, d), lambda l, j: (l, 0, j))],
        out_specs=pl.BlockSpec((None, n, d), lambda l, j: (l, 0, j)),
        out_shape=jax.ShapeDtypeStruct((depth, n, d6), F32),
        compiler_params=_cparams(("parallel", "parallel")),
        name="ada_mod",
    )(c_all, w_ada, b_ada.reshape(depth, 1, d6))


def _mod_spec(mod, col, tile, tiles_per_seq):
    d = mod.shape[-1] // 6
    if mod.ndim == 3:
        return pl.BlockSpec((None, 1, d), lambda i, *_: (i // tiles_per_seq, 0, col))
    return pl.BlockSpec((tile, d), lambda i, *_: (i, col))


def _proj_kernel(x_ref, sh_ref, sc_ref, g_ref, w_ref, q0_ref, q1_ref, k_ref, v_ref, gb_ref, u_ref):
    h = _rms(x_ref[...], g_ref[...]) * (1.0 + sc_ref[...]) + sh_ref[...]
    p = jnp.dot(h.astype(BF16), w_ref[...], preferred_element_type=F32)
    q0_ref[...] = p[:, 0:KV_DIM]
    q1_ref[...] = p[:, KV_DIM:2 * KV_DIM]
    k_ref[...] = p[:, 2 * KV_DIM:3 * KV_DIM]
    v_ref[...] = p[:, 3 * KV_DIM:4 * KV_DIM]
    c0 = 4 * KV_DIM
    cd = (p.shape[1] - c0) // 3
    gb_ref[...] = p[:, c0:c0 + cd]
    u_ref[...] = p[:, c0 + cd:c0 + 2 * cd] * p[:, c0 + 2 * cd:c0 + 3 * cd]


def _proj_in(x, mod, g1, w_in, tile, tiles_per_seq):
    t, d = x.shape
    n_in = w_in.shape[1]
    cd = (n_in - 4 * KV_DIM) // 3
    row = lambda i: (i, 0)
    outs = [jax.ShapeDtypeStruct((t, KV_DIM), F32)] * 4 + [jax.ShapeDtypeStruct((t, cd), F32)] * 2
    out_specs = [pl.BlockSpec((tile, KV_DIM), row)] * 4 + [pl.BlockSpec((tile, cd), row)] * 2
    return pl.pallas_call(
        _proj_kernel,
        grid=(t // tile,),
        in_specs=[pl.BlockSpec((tile, d), row),
                  _mod_spec(mod, 0, tile, tiles_per_seq),
                  _mod_spec(mod, 1, tile, tiles_per_seq),
                  pl.BlockSpec((1, d), lambda i: (0, 0)),
                  pl.BlockSpec((d, n_in), lambda i: (0, 0))],
        out_specs=out_specs,
        out_shape=outs,
        compiler_params=_cparams(("parallel",)),
        name="proj_in",
    )(x, mod, mod, g1, w_in)


def _moba_prompt_kernel(q0_ref, q1_ref, k_ref, v_ref, o_ref, kb_s, vb_s, km4_s, e_s, alibi_s):
    c = pl.program_id(1)
    s_len = k_ref.shape[0]
    nb = s_len // MOBA_BLOCK

    @pl.when(c == 0)
    def _():
        kf = k_ref[...]
        kb_s[...] = kf.astype(BF16)
        vb_s[...] = v_ref[...].astype(BF16)
        km = jnp.sum(kf.reshape(nb, MOBA_BLOCK, KV_DIM), axis=1) * (1.0 / MOBA_BLOCK)
        lane_head = lax.broadcasted_iota(jnp.int32, km.shape, 1) // HEAD_DIM
        parts = [jnp.where(lane_head == hk, km, 0.0) for hk in range(KV_HEADS)]
        parts.append(jnp.zeros((LANES - KV_HEADS * nb, KV_DIM), F32))
        km4_s[...] = jnp.concatenate(parts, axis=0).T
        key_blk = lax.broadcasted_iota(jnp.int32, (LANES, s_len), 1) // MOBA_BLOCK
        row = lax.broadcasted_iota(jnp.int32, (LANES, s_len), 0)
        e_s[...] = ((key_blk == (row & (nb - 1))) & (row < KV_HEADS * nb)).astype(BF16)
        tk = lax.broadcasted_iota(jnp.int32, (1, s_len), 1).astype(F32)
        for h in range(N_HEADS):
            alibi_s[h:h + 1, :] = _alibi_slope(h) * tk

    chunks_per_blk = MOBA_BLOCK // Q_CHUNK
    own = c // chunks_per_blk
    for n_past in range(nb):
        @pl.when(own == n_past)
        def _(n_past=n_past):
            _moba_prompt_chunk(n_past, (c % chunks_per_blk) * Q_CHUNK, q0_ref, q1_ref, o_ref,
                               kb_s, vb_s, km4_s, e_s, alibi_s)


def _block_penalty(q, km4, n_past, nb):
    gate = jnp.dot(q, km4, precision=lax.Precision.HIGHEST, preferred_element_type=F32)
    n_of = lax.broadcasted_iota(jnp.int32, gate.shape, 1) & (nb - 1)
    gate = jnp.where(n_of < n_past, gate, -jnp.inf)
    rank = jnp.zeros(gate.shape, F32)
    for delta in range(1, n_past):
        above = pltpu.roll(gate, LANES - delta, 1)
        rank = rank + jnp.where(above > gate, jnp.where(n_of + delta < nb, 1.0, 0.0), 0.0)
        below = pltpu.roll(gate, delta, 1)
        rank = rank + jnp.where(below >= gate, jnp.where(n_of >= delta, 1.0, 0.0), 0.0)
    return jnp.where(rank < float(MOBA_TOPK), 0.0, NEG_BIG).astype(BF16)


def _moba_prompt_chunk(n_past, q_off, q0_ref, q1_ref, o_ref, kb_s, vb_s, km4_s, e_s, alibi_s):
    nb = kb_s.shape[0] // MOBA_BLOCK
    n_keys = (n_past + 1) * MOBA_BLOCK
    past_keys = n_past * MOBA_BLOCK
    q_pos = q_off + lax.broadcasted_iota(jnp.int32, (Q_CHUNK, MOBA_BLOCK), 0)
    causal = jnp.where(q_pos >= lax.broadcasted_iota(jnp.int32, (Q_CHUNK, MOBA_BLOCK), 1), 0.0, NEG_BIG)
    causal = jnp.concatenate([causal] * KV_HEADS, axis=0)
    lane_head = lax.broadcasted_iota(jnp.int32, (Q_CHUNK, KV_DIM), 1) // HEAD_DIM
    need_gate = n_past > MOBA_TOPK
    if need_gate:
        km4 = km4_s[...]
        lane_grp = lax.broadcasted_iota(jnp.int32, (Q_CHUNK, LANES), 1) // nb
    kb = kb_s[0:n_keys, :]
    vb = vb_s[0:n_keys, :]
    outs = [None] * N_HEADS
    for g, q_ref in enumerate((q0_ref, q1_ref)):
        q = q_ref[...]
        qpad = jnp.concatenate([jnp.where(lane_head == hk, q, 0.0) for hk in range(KV_HEADS)], axis=0)
        s = lax.dot_general(qpad.astype(BF16), kb, NT_DIMS, preferred_element_type=F32)
        st = s + jnp.concatenate(
            [jnp.broadcast_to(alibi_s[hk * Q_GROUP + g:hk * Q_GROUP + g + 1, 0:n_keys], (Q_CHUNK, n_keys))
             for hk in range(KV_HEADS)], axis=0)
        if need_gate:
            pen = _block_penalty(q, km4, n_past, nb)
            pen = jnp.concatenate([jnp.where(lane_grp == hk, pen, jnp.zeros((), BF16))
                                   for hk in range(KV_HEADS)], axis=0)
            bias = jnp.dot(pen, e_s[:, 0:past_keys], preferred_element_type=F32)
            st = jnp.concatenate([st[:, :past_keys] + bias, st[:, past_keys:] + causal], axis=1)
        elif n_past > 0:
            st = jnp.concatenate([st[:, :past_keys], st[:, past_keys:] + causal], axis=1)
        else:
            st = st + causal
        mx = jnp.max(st, axis=1, keepdims=True)
        p = jnp.exp(st - mx)
        l = jnp.sum(p, axis=1, keepdims=True)
        o = jnp.dot(p.astype(BF16), vb, preferred_element_type=F32) / l
        for hk in range(KV_HEADS):
            outs[hk * Q_GROUP + g] = o[hk * Q_CHUNK:(hk + 1) * Q_CHUNK, hk * HEAD_DIM:(hk + 1) * HEAD_DIM]
    o_ref[...] = jnp.concatenate(outs, axis=1)


def _moba_prompt(q0, q1, k, v, batch, seq):
    n_chunks = seq // Q_CHUNK
    nb = seq // MOBA_BLOCK
    assert nb & (nb - 1) == 0 and KV_HEADS * nb <= LANES
    qspec = pl.BlockSpec((Q_CHUNK, KV_DIM), lambda b, c: (b * n_chunks + c, 0))
    kvspec = pl.BlockSpec((seq, KV_DIM), lambda b, c: (b, 0))
    return pl.pallas_call(
        _moba_prompt_kernel,
        grid=(batch, n_chunks),
        in_specs=[qspec, qspec, kvspec, kvspec],
        out_specs=pl.BlockSpec((Q_CHUNK, ATTN_DIM), lambda b, c: (b * n_chunks + c, 0)),
        out_shape=jax.ShapeDtypeStruct((batch * seq, ATTN_DIM), F32),
        scratch_shapes=[pltpu.VMEM((seq, KV_DIM), BF16), pltpu.VMEM((seq, KV_DIM), BF16),
                        pltpu.VMEM((KV_DIM, LANES), F32), pltpu.VMEM((LANES, seq), BF16),
                        pltpu.VMEM((N_HEADS, seq), F32)],
        compiler_params=_cparams(("parallel", "arbitrary")),
        name="moba_prompt",
    )(q0, q1, k, v)


def _moba_decode_kernel(pt_ref, qs_ref, kn_ref, vn_ref, *refs, pps, n_blk, pos0, page):
    del pt_ref
    kp = refs[:pps]
    vp = refs[pps:2 * pps]
    o_ref = refs[2 * pps]
    m_s, l_s, km_s, o_s = refs[2 * pps + 1:]
    step = pl.program_id(1)
    n_rows = KV_HEADS * qs_ref.shape[0]
    sd = qs_ref.shape[0] // Q_GROUP
    ppb = MOBA_BLOCK // page
    bps = pps // ppb

    qs = qs_ref[...]
    lane_head = lax.broadcasted_iota(jnp.int32, qs.shape, 1) // HEAD_DIM
    qpad = jnp.concatenate([jnp.where(lane_head == hk, qs, 0.0) for hk in range(KV_HEADS)], axis=0)
    qpad_b = qpad.astype(BF16)
    row = lax.broadcasted_iota(jnp.int32, (n_rows, 1), 0)
    head = (row // qs.shape[0]) * Q_GROUP + (row % qs.shape[0]) // sd
    qi = row % sd
    slope = jnp.zeros((n_rows, 1), F32)
    for h in range(N_HEADS):
        slope = jnp.where(head == h, _alibi_slope(h), slope)
    lane = lax.broadcasted_iota(jnp.int32, (n_rows, LANES), 1)

    @pl.when(step == 0)
    def _():
        m_s[...] = jnp.zeros_like(m_s)
        l_s[...] = jnp.zeros_like(l_s)
        km_s[...] = jnp.zeros_like(km_s)

    kcol = lax.broadcasted_iota(jnp.int32, (KV_DIM, LANES), 1)
    for n in range(bps):
        blk = step * bps + n
        sts = []
        ksum = jnp.zeros((KV_DIM, 1), F32)
        for i in range(ppb):
            kt = kp[n * ppb + i][...]
            s = jnp.dot(qpad_b, kt.astype(BF16), preferred_element_type=F32)
            tk = blk * MOBA_BLOCK + i * page + lax.broadcasted_iota(jnp.int32, (1, page), 1)
            sts.append(s - slope * ((pos0 + qi) - tk).astype(F32))
            ksum = ksum + jnp.sum(kt, axis=1, keepdims=True)
        mx = sts[0].max(axis=1, keepdims=True)
        for st in sts[1:]:
            mx = jnp.maximum(mx, st.max(axis=1, keepdims=True))
        l = jnp.zeros_like(mx)
        o = jnp.zeros((n_rows, KV_DIM), F32)
        for i, st in enumerate(sts):
            p = jnp.exp(st - mx)
            l = l + jnp.sum(p, axis=1, keepdims=True)
            o = o + lax.dot_general(p.astype(BF16), vp[n * ppb + i][...].astype(BF16), NT_DIMS,
                                    preferred_element_type=F32)
        o_s[blk] = o
        here = lane == blk
        m_s[...] = jnp.where(here, mx, m_s[...])
        l_s[...] = jnp.where(here, l, l_s[...])
        km_s[...] = jnp.where(kcol == blk, ksum * (1.0 / MOBA_BLOCK), km_s[...])

    @pl.when(step == pl.num_programs(1) - 1)
    def _():
        valid = lane < n_blk
        gates = jnp.dot(qpad, km_s[...], precision=lax.Precision.HIGHEST, preferred_element_type=F32)
        work = jnp.where(valid, gates, -jnp.inf)
        sel = jnp.zeros((n_rows, LANES), F32)
        for _ in range(min(MOBA_TOPK, n_blk)):
            top = jnp.max(work, axis=1, keepdims=True)
            first = jnp.min(jnp.where(work == top, lane, LANES), axis=1, keepdims=True)
            hit = (lane == first) & valid
            sel = jnp.where(hit, 1.0, sel)
            work = jnp.where(hit, -jnp.inf, work)
        picked = sel > 0.0
        own_scores = []
        for j in range(sd):
            sj = jnp.sum(qpad * kn_ref[j:j + 1, :], axis=1, keepdims=True)
            dj = (qi - j).astype(F32)
            own_scores.append(jnp.where(dj >= 0.0, sj - slope * dj, NEG_BIG))
        m_tot = jnp.max(jnp.where(picked, m_s[...], NEG_BIG), axis=1, keepdims=True)
        for sj in own_scores:
            m_tot = jnp.maximum(m_tot, sj)
        wn = jnp.where(picked, jnp.exp(m_s[...] - m_tot), 0.0)
        denom = jnp.sum(wn * l_s[...], axis=1, keepdims=True)
        acc = jnp.zeros((n_rows, KV_DIM), F32)
        for j, sj in enumerate(own_scores):
            pj = jnp.exp(sj - m_tot)
            denom = denom + pj
            acc = acc + pj * vn_ref[j:j + 1, :]
        for n in range(n_blk):
            coef = jnp.sum(jnp.where(lane == n, wn, 0.0), axis=1, keepdims=True)
            acc = acc + coef * o_s[n]
        o_ref[...] = acc / denom


def _moba_decode(qs, k_new, v_new, cache_k, cache_v, page_table, layer):
    bd, rows8, _ = qs.shape
    sd = k_new.shape[1]
    n_pages = page_table.shape[1]
    page = cache_k.shape[2]
    n_pool = cache_k.shape[1]
    ppb = MOBA_BLOCK // page
    n_blk = n_pages // ppb
    assert n_pages % ppb == 0 and n_blk <= LANES
    pps = min(16, n_pages)
    assert n_pages % pps == 0 and pps % ppb == 0
    ck = cache_k.transpose(0, 1, 3, 4, 2).reshape(cache_k.shape[0], n_pool, KV_DIM, page)
    cv = cache_v.transpose(0, 1, 3, 4, 2).reshape(cache_v.shape[0], n_pool, KV_DIM, page)
    n_rows = KV_HEADS * rows8

    def page_spec(i):
        return pl.BlockSpec((None, None, KV_DIM, page),
                            lambda b, s, pt: (layer, pt[b, s * pps + i], 0, 0))

    seq_spec = lambda r: pl.BlockSpec((None, r, KV_DIM), lambda b, s, pt: (b, 0, 0))
    grid_spec = pltpu.PrefetchScalarGridSpec(
        num_scalar_prefetch=1,
        grid=(bd, n_pages // pps),
        in_specs=[seq_spec(rows8), seq_spec(sd), seq_spec(sd)]
        + [page_spec(i) for i in range(pps)] * 2,
        out_specs=seq_spec(n_rows),
        scratch_shapes=[pltpu.VMEM((n_rows, LANES), F32)] * 2
        + [pltpu.VMEM((KV_DIM, LANES), F32), pltpu.VMEM((n_blk, n_rows, KV_DIM), F32)],
    )
    kern = functools.partial(_moba_decode_kernel, pps=pps, n_blk=n_blk,
                             pos0=n_pages * page, page=page)
    return pl.pallas_call(
        kern,
        grid_spec=grid_spec,
        out_shape=jax.ShapeDtypeStruct((bd, n_rows, KV_DIM), F32),
        compiler_params=_cparams(("parallel", "arbitrary")),
        name="moba_decode",
    )(page_table, qs, k_new, v_new, *([ck] * pps), *([cv] * pps))


def _mix_kernel(*refs, decode, tiles_per_seq):
    if decode:
        (x_ref, attn_ref, gb_ref, u_ref, u1_ref, u2_ref, ga1_ref, sh2_ref, sc2_ref, cw_ref, ga_ref,
         gc_ref, wo_ref, g2_ref, wq_ref, sk_ref, xo_ref, h2_ref, st_ref) = refs
        u0 = u_ref[...]
        u1 = u1_ref[...]
        u2 = u2_ref[...]
    else:
        (x_ref, attn_ref, gb_ref, u_ref, halo_ref, ga1_ref, sh2_ref, sc2_ref, cw_ref, ga_ref,
         gc_ref, wo_ref, g2_ref, wq_ref, sk_ref, xo_ref, h2_ref, st_ref) = refs
        u0 = u_ref[...]
        first = pl.program_id(0) % tiles_per_seq == 0
        halo = jnp.where(first, 0.0, halo_ref[...])
        hr = halo.shape[0]
        rowi = lax.broadcasted_iota(jnp.int32, u0.shape, 0)
        u1 = jnp.where(rowi < 1, halo[hr - 1:hr, :], pltpu.roll(u0, 1, 0))
        u2 = jnp.where(rowi < 1, halo[hr - 2:hr - 1, :],
                       jnp.where(rowi < 2, halo[hr - 1:hr, :], pltpu.roll(u0, 2, 0)))
    cw = cw_ref[...]
    y = cw[0:1, :] * u2
    y = y + cw[1:2, :] * u1
    y = y + cw[2:3, :] * u0
    yc = gb_ref[...] * y
    cat = jnp.concatenate([_rms(attn_ref[...], ga_ref[...]), _rms(yc, gc_ref[...])], axis=1)
    mixed = jnp.dot(cat.astype(BF16), wo_ref[...], preferred_element_type=F32)
    x_new = x_ref[...] + ga1_ref[...] * mixed
    xo_ref[...] = x_new
    h2 = _rms(x_new, g2_ref[...]) * (1.0 + sc2_ref[...]) + sh2_ref[...]
    h_hi = h2.astype(BF16)
    h2_ref[...] = h_hi
    h_lo = (h2 - h_hi.astype(F32)).astype(BF16)
    q = jnp.dot(h_hi, wq_ref[0], preferred_element_type=F32) + (
        jnp.dot(h_hi, wq_ref[1], preferred_element_type=F32)
        + jnp.dot(h_lo, wq_ref[0], preferred_element_type=F32))
    qt = q.T
    qt_hi = qt.astype(BF16)
    qt_lo = (qt - qt_hi.astype(F32)).astype(BF16)
    for h in range(PEER_HEADS):
        for p in range(2):
            rows = slice((h * 2 + p) * PEER_HALF, (h * 2 + p + 1) * PEER_HALF)
            st_ref[h, p] = jnp.dot(sk_ref[0, p], qt_hi[rows, :], preferred_element_type=F32) + (
                jnp.dot(sk_ref[0, p], qt_lo[rows, :], preferred_element_type=F32)
                + jnp.dot(sk_ref[1, p], qt_hi[rows, :], preferred_element_type=F32))


def _mix(x, attn, gb, u, shifted, mod, conv_w, g_attn, g_conv, w_o, g2, w_query, sub_keys,
         tile, tiles_per_seq):
    t, d = x.shape
    cd = u.shape[1]
    decode = shifted is not None
    row = lambda i: (i, 0)
    full2 = lambda a: pl.BlockSpec(a.shape, lambda i: (0,) * a.ndim)
    if decode:
        conv_in = [u, shifted[0], shifted[1]]
        conv_specs = [pl.BlockSpec((tile, cd), row)] * 3
    else:
        halo_rows = 8
        conv_in = [u, u]
        conv_specs = [pl.BlockSpec((tile, cd), row),
                      pl.BlockSpec((halo_rows, cd),
                                   lambda i: (jnp.maximum(i * (tile // halo_rows) - 1, 0), 0))]
    kern = functools.partial(_mix_kernel, decode=decode, tiles_per_seq=tiles_per_seq)
    return pl.pallas_call(
        kern,
        grid=(t // tile,),
        in_specs=[pl.BlockSpec((tile, d), row), pl.BlockSpec((tile, attn.shape[1]), row),
                  pl.BlockSpec((tile, cd), row)] + conv_specs
        + [_mod_spec(mod, 2, tile, tiles_per_seq), _mod_spec(mod, 3, tile, tiles_per_seq),
           _mod_spec(mod, 4, tile, tiles_per_seq),
           full2(conv_w), full2(g_attn), full2(g_conv), full2(w_o), full2(g2), full2(w_query),
           full2(sub_keys)],
        out_specs=[pl.BlockSpec((tile, d), row), pl.BlockSpec((tile, d), row),
                   pl.BlockSpec((PEER_HEADS, 2, PEER_KEYS, tile), lambda i: (0, 0, 0, i))],
        out_shape=[jax.ShapeDtypeStruct((t, d), F32), jax.ShapeDtypeStruct((t, d), BF16),
                   jax.ShapeDtypeStruct((PEER_HEADS, 2, PEER_KEYS, t), F32)],
        compiler_params=_cparams(("parallel",)),
        name="mix_scores",
    )(x, attn, gb, *conv_in, mod, mod, mod, conv_w, g_attn, g_conv, w_o, g2, w_query, sub_keys)


def _top_rows(work, n_take, vals_ref):
    n = work.shape[0]
    idx = lax.broadcasted_iota(jnp.int32, work.shape, 0).astype(F32)
    rank = jnp.full(work.shape, float(n_take), F32)
    for r in range(n_take):
        top = jnp.max(work, axis=0, keepdims=True)
        first = jnp.min(jnp.where(work == top, idx, float(n)), axis=0, keepdims=True)
        hit = idx == first
        rank = jnp.where(hit, float(r), rank)
        work = jnp.where(hit, -jnp.inf, work)
        if vals_ref is not None:
            vals_ref[r:r + 1, :] = top
    return rank


def _top_rows_distinct(work, n_take, vals_ref, want_rank=True):
    rank = jnp.full(work.shape, float(n_take), F32) if want_rank else None
    for r in range(n_take):
        top = jnp.max(work, axis=0, keepdims=True)
        hit = work == top
        if want_rank:
            rank = jnp.where(hit, float(r), rank)
        work = jnp.where(hit, -jnp.inf, work)
        if vals_ref is not None:
            vals_ref[r:r + 1, :] = top
    taken = rank < float(n_take) if want_rank else work == -jnp.inf
    n_marked = jnp.sum(jnp.where(taken, 1.0, 0.0), axis=0, keepdims=True)
    return rank, n_marked


def _dup_bf16(x):
    u = lax.bitcast_convert_type(x, jnp.uint32)
    r = (u + jnp.uint32(0x7FFF) + ((u >> 16) & jnp.uint32(1))) >> 16
    return r | (r << 16)


def _write_tables(h, refs, s1, s2, rank2, c1, z, a0, b0):
    c1_ref, e1_ref, r2_ref, e2_ref = refs
    c1_ref[h] = _dup_bf16(c1)
    e1_ref[h] = _dup_bf16(jnp.exp(s1 - a0) / z)
    r2_ref[h] = pltpu.bitcast(rank2.astype(BF16), jnp.uint32)
    e2_ref[h] = pltpu.bitcast(jnp.exp(s2 - b0).astype(BF16), jnp.uint32)


def _route_head_exact(h, s_ref, refs, a_s, b_s):
    s1 = s_ref[h, 0]
    s2 = s_ref[h, 1]
    rank1 = _top_rows(s1, PEER_TOPK, a_s)
    rank2 = _top_rows(s2, PEER_TOPK, b_s)
    a = a_s[...]
    b = b_s[...]
    cand = jnp.concatenate([a[r:r + 1, :] + b for r in range(PEER_TOPK)], axis=0)
    crank = _top_rows(cand, PEER_TOPK, None)
    sel = crank < float(PEER_TOPK)
    top0 = a[0:1, :] + b[0:1, :]
    z = jnp.sum(jnp.where(sel, jnp.exp(cand - top0), 0.0), axis=0, keepdims=True)
    self = jnp.where(sel, 1.0, 0.0)
    c1 = jnp.zeros_like(s1)
    for r in range(PEER_TOPK):
        cnt = jnp.sum(self[r * PEER_TOPK:(r + 1) * PEER_TOPK, :], axis=0, keepdims=True)
        c1 = jnp.where(rank1 == float(r), cnt, c1)
    _write_tables(h, refs, s1, s2, rank2, c1, z, a[0:1, :], b[0:1, :])


def _route_head_distinct(h, s_ref, refs, a_s, b_s):
    s1 = s_ref[h, 0]
    s2 = s_ref[h, 1]
    _, n1 = _top_rows_distinct(s1, PEER_TOPK, a_s, want_rank=False)
    rank2, n2 = _top_rows_distinct(s2, PEER_TOPK, b_s)
    a = a_s[...]
    b = b_s[...]
    half = PEER_TOPK // 2
    cand = jnp.concatenate([a[0:1, :] + b] + [a[r:r + 1, :] + b[0:half, :] for r in range(1, half)]
                           + [a[half:, :] + b[0:1, :]], axis=0)
    row = lax.broadcasted_iota(jnp.int32, cand.shape, 0)
    assert half == 8
    r_mid = ((row - PEER_TOPK) >> 3) + 1
    c_mid = (row - PEER_TOPK) & 7
    in_mid = (row >= PEER_TOPK) & (row < PEER_TOPK + half * (half - 1))
    cand = jnp.where(in_mid & ((r_mid + 1) * (c_mid + 1) > PEER_TOPK), -jnp.inf, cand)
    crank, nc = _top_rows_distinct(cand, PEER_TOPK, None)
    sel = crank < float(PEER_TOPK)
    top0 = a[0:1, :] + b[0:1, :]
    z = jnp.sum(jnp.where(sel, jnp.exp(cand - top0), 0.0), axis=0, keepdims=True)
    self = jnp.where(sel, 1.0, 0.0)
    cnts = [jnp.sum(self[0:PEER_TOPK, :], axis=0, keepdims=True)]
    for r in range(1, half):
        lo = PEER_TOPK + (r - 1) * half
        cnts.append(jnp.sum(self[lo:lo + half, :], axis=0, keepdims=True))
    tail = PEER_TOPK + half * (half - 1)
    cnts += [self[tail + r:tail + r + 1, :] for r in range(half)]
    c1 = jnp.zeros_like(s1)
    for r in range(PEER_TOPK):
        c1 = jnp.where(s1 == a[r:r + 1, :], cnts[r], c1)
    _write_tables(h, refs, s1, s2, rank2, c1, z, a[0:1, :], b[0:1, :])
    want = float(PEER_TOPK)
    return jnp.abs(n1 - want) + jnp.abs(n2 - want) + jnp.abs(nc - want)


def _route_kernel(s_ref, c1_ref, e1_ref, r2_ref, e2_ref, a_s, b_s):
    refs = (c1_ref, e1_ref, r2_ref, e2_ref)

    def head_pair(i, carry):
        heads = (2 * i, 2 * i + 1)
        ties = [_route_head_distinct(h, s_ref, refs, a_s.at[k], b_s.at[k]) for k, h in enumerate(heads)]
        for k, h in enumerate(heads):
            @pl.when(jnp.max(ties[k]) > 0.0)
            def _(k=k, h=h):
                _route_head_exact(h, s_ref, refs, a_s.at[k], b_s.at[k])

        return carry

    lax.fori_loop(0, PEER_HEADS // 2, head_pair, 0)


def _route(scores):
    _, _, nk, t = scores.shape
    tile = LANES
    wspec = pl.BlockSpec((PEER_HEADS, nk, tile), lambda i: (0, 0, i))
    pspec = pl.BlockSpec((PEER_HEADS, nk // 2, tile), lambda i: (0, 0, i))
    words = jax.ShapeDtypeStruct((PEER_HEADS, nk, t), jnp.uint32)
    packed = jax.ShapeDtypeStruct((PEER_HEADS, nk // 2, t), jnp.uint32)
    return pl.pallas_call(
        _route_kernel,
        grid=(t // tile,),
        in_specs=[pl.BlockSpec((PEER_HEADS, 2, nk, tile), lambda i: (0, 0, 0, i))],
        out_specs=[wspec, wspec, pspec, pspec],
        out_shape=[words, words, packed, packed],
        scratch_shapes=[pltpu.VMEM((2, PEER_TOPK, tile), F32)] * 2,
        compiler_params=_cparams(("parallel",)),
        name="peer_route",
    )(scores)


def _gelu(x):
    return 0.5 * x * (1.0 + lax.erf(x * (2.0 ** -0.5)))


def _peer_kernel(h2_ref, u_ref, vt_ref, c1_ref, e1_ref, r2_ref, e2_ref, x_ref, ga2_ref, gf_ref,
                 o_ref, acc_s, w_a, w_b, ht_a, ht_b, *, final):
    j = pl.program_id(1)
    n_e = pl.num_programs(1) - 1
    slot = j % 2

    @pl.when(j == 0)
    def _():
        acc_s[...] = jnp.zeros_like(acc_s)

    def expert_tile(w_prev, w_cur, first=False, last=False):
        xb = h2_ref[...]
        tt = xb.shape[0]
        grp = PEER_KEYS // BF16_ROWS
        pair = 2 * PEER_KEYS
        n_stages = u_ref.shape[0] // pair
        v_rows = vt_ref.shape[0] // n_stages
        hts = (ht_a, ht_b)

        def scores(k):
            rows = slice(k * pair, (k + 1) * pair)
            hts[k % 2][...] = lax.dot_general(u_ref[rows, :], xb, NT_DIMS, preferred_element_type=F32)

        def second_product(k):
            rows = slice(k * v_rows, (k + 1) * v_rows)
            acc_s[rows, :] += jnp.dot(vt_ref[rows, :], w_prev[...], preferred_element_type=F32)

        def activations(k):
            ht_ref = hts[k % 2]
            c1_rows = [[pltpu.bitcast(jnp.broadcast_to(c1_ref[h, 2 * k + sub:2 * k + sub + 1, :], (8, tt)), BF16)
                        for h in range(PEER_HEADS)] for sub in range(2)]
            e1_rows = [[pltpu.bitcast(jnp.broadcast_to(e1_ref[h, 2 * k + sub:2 * k + sub + 1, :], (8, tt)), BF16)
                        for h in range(PEER_HEADS)] for sub in range(2)]
            for cb in range(tt // LANES):
                cols = slice(cb * LANES, (cb + 1) * LANES)
                gates = [jnp.zeros((PEER_KEYS, LANES), BF16) for _ in range(2)]
                for h in range(PEER_HEADS):
                    r2 = pltpu.bitcast(r2_ref[h, :, cols], BF16)
                    e2 = pltpu.bitcast(e2_ref[h, :, cols], BF16)
                    for sub in range(2):
                        c1 = jnp.concatenate([c1_rows[sub][h][:, cols]] * grp, axis=0)
                        e1 = jnp.concatenate([e1_rows[sub][h][:, cols]] * grp, axis=0)
                        gates[sub] = gates[sub] + jnp.where(r2 < c1, e2, jnp.zeros((), BF16)) * e1
                for sub in range(2):
                    r0 = k * pair + sub * PEER_KEYS
                    act = _gelu(ht_ref[sub * PEER_KEYS:(sub + 1) * PEER_KEYS, cols]).astype(BF16)
                    w_cur[r0:r0 + PEER_KEYS, cols] = act * gates[sub]

        if last:
            for k in range(n_stages):
                second_product(k)
            return
        scores(0)
        for k in range(1, n_stages):
            activations(k - 1)
            scores(k)
            if not first:
                second_product(k - 1)
        activations(n_stages - 1)
        if not first:
            second_product(n_stages - 1)

    @pl.when(j == 0)
    def _():
        expert_tile(None, w_a, first=True)

    middle = (j > 0) & (j < n_e)

    @pl.when(middle & (slot == 0))
    def _():
        expert_tile(w_b, w_a)

    @pl.when(middle & (slot == 1))
    def _():
        expert_tile(w_a, w_b)

    @pl.when((j == n_e) & (slot == 0))
    def _():
        expert_tile(w_b, None, last=True)

    @pl.when((j == n_e) & (slot == 1))
    def _():
        expert_tile(w_a, None, last=True)

    @pl.when(j == n_e)
    def _():
        out = x_ref[...] + ga2_ref[...] * acc_s[...].T
        if final:
            out = _rms(out, gf_ref[...])
        o_ref[...] = out


def _peer(h2, u_tab, vt_tab, tabs, x_new, mod, g_final, tile, tiles_per_seq, final):
    t, d = x_new.shape
    n_exp = u_tab.shape[0]
    e_tile = min(PEER_EXPERT_TILE, n_exp)
    n_stages = e_tile // (2 * PEER_KEYS)
    assert n_exp % e_tile == 0 and e_tile % (2 * PEER_KEYS) == 0 and d % (n_stages * BF16_ROWS) == 0
    n_e = n_exp // e_tile
    c1, e1, r2, e2 = tabs
    nk = c1.shape[1]
    row = lambda i, j: (i, 0)
    cur = lambda j: jnp.minimum(j, n_e - 1)
    i1_spec = pl.BlockSpec((PEER_HEADS, e_tile // nk, tile), lambda i, j: (0, cur(j), i))
    i2_spec = pl.BlockSpec((PEER_HEADS, nk // 2, tile), lambda i, j: (0, 0, i))
    return pl.pallas_call(
        functools.partial(_peer_kernel, final=final),
        grid=(t // tile, n_e + 1),
        in_specs=[pl.BlockSpec((tile, d), row),
                  pl.BlockSpec((e_tile, d), lambda i, j: (cur(j), 0)),
                  pl.BlockSpec((d, e_tile), lambda i, j: (0, jnp.maximum(j - 1, 0))),
                  i1_spec, i1_spec, i2_spec, i2_spec,
                  pl.BlockSpec((tile, d), row),
                  _mod_spec(mod, 5, tile, tiles_per_seq),
                  pl.BlockSpec((1, d), lambda i, j: (0, 0))],
        out_specs=pl.BlockSpec((tile, d), row),
        out_shape=jax.ShapeDtypeStruct((t, d), F32),
        scratch_shapes=[pltpu.VMEM((d, tile), F32),
                        pltpu.VMEM((e_tile, tile), BF16), pltpu.VMEM((e_tile, tile), BF16),
                        pltpu.VMEM((2 * PEER_KEYS, tile), F32), pltpu.VMEM((2 * PEER_KEYS, tile), F32)],
        compiler_params=_cparams(("parallel", "arbitrary")),
        name="peer_dense",
    )(h2, u_tab, vt_tab, c1, e1, r2, e2, x_new, mod, g_final)


def _split_bf16(w):
    hi = w.astype(BF16)
    return jnp.stack([hi, (w - hi.astype(F32)).astype(BF16)])


def _token_tile(seq, want):
    tile = min(want, seq)
    assert seq % tile == 0
    return tile


def _layer(x, mod, past, conv_prev, weights, batch, seq, layer, final):
    (g1, w_in, conv_w, g_attn, g_conv, w_o, g2, w_query, sub_keys, u_tab, vt_tab, g_final) = weights
    decode = past is not None
    t, d = x.shape
    if decode:
        tile, tps = t, 1
    else:
        tile = _token_tile(seq, 512)
        tps = seq // tile
    q0, q1, k, v, gb, u = _proj_in(x, mod, g1, w_in, tile, tps)
    cd = u.shape[1]
    if decode:
        cache_k, cache_v, page_table = past
        qs = jnp.concatenate([q0.reshape(batch, seq, KV_DIM), q1.reshape(batch, seq, KV_DIM)], axis=1)
        o = _moba_decode(qs, k.reshape(batch, seq, KV_DIM), v.reshape(batch, seq, KV_DIM),
                         cache_k, cache_v, page_table, layer)
        o = o.reshape(batch, KV_HEADS, Q_GROUP, seq, KV_HEADS, HEAD_DIM)
        o = jnp.stack([o[:, hk, :, :, hk, :] for hk in range(KV_HEADS)], axis=1)
        attn = o.transpose(0, 3, 1, 2, 4).reshape(t, ATTN_DIM)
        xp = jnp.concatenate([conv_prev, u.reshape(batch, seq, cd)], axis=1)
        shifted = (xp[:, 1:1 + seq].reshape(t, cd), xp[:, 0:seq].reshape(t, cd))
        conv_new = xp[:, -(CONV_WIDTH - 1):]
    else:
        attn = _moba_prompt(q0, q1, k, v, batch, seq)
        shifted = None
        conv_new = u.reshape(batch, seq, cd)[:, -(CONV_WIDTH - 1):]
    x_new, h2, scores = _mix(x, attn, gb, u, shifted, mod, conv_w, g_attn, g_conv, w_o, g2,
                             w_query, sub_keys, tile, tps)
    tabs = _route(scores)
    x_out = _peer(h2, u_tab, vt_tab, tabs, x_new, mod, g_final, tile, tps, final)
    return x_out, k, v, conv_new


def kernel(x_prompt, x_sample, cache_k, cache_v, state_conv, page_table, c_prompt, c_sample,
           w_ada, b_ada, g_norm1, g_norm2, w_in, conv_w, g_out_attn, g_out_conv, w_o,
           w_query, sub_keys, u_experts, v_experts, g_final):
    b, s, d = x_prompt.shape
    bd, sd, _ = x_sample.shape
    depth = w_ada.shape[0]
    assert s % MOBA_BLOCK == 0 and (bd * sd) % LANES == 0

    mod = _ada_mod(jnp.concatenate([c_prompt, c_sample], axis=0), w_ada, b_ada)

    scale = HEAD_DIM ** -0.5
    cols0 = jnp.concatenate([jnp.arange(HEAD_DIM) + (hk * Q_GROUP) * HEAD_DIM for hk in range(KV_HEADS)])
    cols1 = cols0 + HEAD_DIM

    xp = x_prompt.reshape(b * s, d)
    xs = x_sample.reshape(bd * sd, d)
    zero_conv = jnp.zeros((b, CONV_WIDTH - 1, conv_w.shape[-1]), x_prompt.dtype)
    ks_p, vs_p, cs_p, ks_s, vs_s, cs_s = [], [], [], [], [], []
    for l in range(depth):
        wl = w_in[l]
        w_in_l = jnp.concatenate([wl[:, cols0] * scale, wl[:, cols1] * scale, wl[:, ATTN_DIM:]],
                                 axis=1).astype(BF16)
        weights = (g_norm1[l][None], w_in_l, conv_w[l], g_out_attn[l][None], g_out_conv[l][None],
                   w_o[l].astype(BF16), g_norm2[l][None], _split_bf16(w_query[l]), _split_bf16(sub_keys[l]),
                   u_experts[l].astype(BF16), v_experts[l].T.astype(BF16), g_final[None])
        final = l == depth - 1
        mod_p = mod[l, :b][:, None, :]
        mod_s = jnp.repeat(mod[l, b:], sd, axis=0)
        xp, k, v, cn = _layer(xp, mod_p, None, zero_conv, weights, b, s, l, final)
        ks_p.append(k.reshape(b, s, KV_HEADS, HEAD_DIM))
        vs_p.append(v.reshape(b, s, KV_HEADS, HEAD_DIM))
        cs_p.append(cn)
        xs, k, v, cn = _layer(xs, mod_s, (cache_k, cache_v, page_table), state_conv[l], weights,
                              bd, sd, l, final)
        ks_s.append(k.reshape(bd, sd, KV_HEADS, HEAD_DIM))
        vs_s.append(v.reshape(bd, sd, KV_HEADS, HEAD_DIM))
        cs_s.append(cn)
    return (xp.reshape(b, s, d), xs.reshape(bd, sd, d),
            jnp.stack(ks_p), jnp.stack(vs_p), jnp.stack(cs_p),
            jnp.stack(ks_s), jnp.stack(vs_s), jnp.stack(cs_s))
```

```python
import functools

import jax
import jax.numpy as jnp
from jax import lax
from jax.experimental import pallas as pl
from jax.experimental.pallas import tpu as pltpu

F32 = jnp.float32
BF16 = jnp.bfloat16

N_HEADS = 8
KV_HEADS = 4
HEAD_DIM = 64
Q_GROUP = N_HEADS // KV_HEADS
ATTN_DIM = N_HEADS * HEAD_DIM
KV_DIM = KV_HEADS * HEAD_DIM
MOBA_BLOCK = 256
MOBA_TOPK = 3
Q_CHUNK = 128
CONV_WIDTH = 3
PEER_HEADS = 8
PEER_KEYS = 128
PEER_HALF = 64
PEER_TOPK = 16
NORM_EPS = 1e-6
NEG_BIG = -1e30

VMEM_LIMIT_BYTES = 56 * 1024 * 1024
LANES = 128
BF16_ROWS = 16
PEER_EXPERT_TILE = 2048

NT_DIMS = (((1,), (1,)), ((), ()))


def _cparams(sem):
    return pltpu.CompilerParams(dimension_semantics=sem, vmem_limit_bytes=VMEM_LIMIT_BYTES)


def _rms(x, g):
    y = x * lax.rsqrt(jnp.mean(x * x, axis=-1, keepdims=True) + NORM_EPS)
    return y * g


def _alibi_slope(h):
    return 2.0 ** (-8.0 * (h + 1) / N_HEADS)


def _ada_kernel(c_ref, w_ref, b_ref, o_ref):
    c = c_ref[...]
    o_ref[...] = jnp.dot(c * jax.nn.sigmoid(c), w_ref[...], precision=lax.Precision.HIGHEST,
                         preferred_element_type=F32) + b_ref[...]


def _ada_mod(c_all, w_ada, b_ada):
    depth, d, d6 = w_ada.shape
    n = c_all.shape[0]
    return pl.pallas_call(
        _ada_kernel,
        grid=(depth, d6 // d),
        in_specs=[pl.BlockSpec((n, d), lambda l, j: (0, 0)),
                  pl.BlockSpec((None, d, d), lambda l, j: (l, 0, j)),
                  pl.BlockSpec((None, 1, d), lambda l, j: (l, 0, j))],
        out_specs=pl.BlockSpec((None, n, d), lambda l, j: (l, 0, j)),
        out_shape=jax.ShapeDtypeStruct((depth, n, d6), F32),
        compiler_params=_cparams(("parallel", "parallel")),
        name="ada_mod",
    )(c_all, w_ada, b_ada.reshape(depth, 1, d6))


def _mod_spec(mod, col, tile, tiles_per_seq):
    d = mod.shape[-1] // 6
    if mod.ndim == 3:
        return pl.BlockSpec((None, 1, d), lambda i, *_: (i // tiles_per_seq, 0, col))
    return pl.BlockSpec((tile, d), lambda i, *_: (i, col))


def _proj_kernel(x_ref, sh_ref, sc_ref, g_ref, w_ref, q0_ref, q1_ref, k_ref, v_ref, gb_ref, u_ref):
    h = _rms(x_ref[...], g_ref[...]) * (1.0 + sc_ref[...]) + sh_ref[...]
    p = jnp.dot(h.astype(BF16), w_ref[...], preferred_element_type=F32)
    q0_ref[...] = p[:, 0:KV_DIM]
    q1_ref[...] = p[:, KV_DIM:2 * KV_DIM]
    k_ref[...] = p[:, 2 * KV_DIM:3 * KV_DIM]
    v_ref[...] = p[:, 3 * KV_DIM:4 * KV_DIM]
    c0 = 4 * KV_DIM
    cd = (p.shape[1] - c0) // 3
    gb_ref[...] = p[:, c0:c0 + cd]
    u_ref[...] = p[:, c0 + cd:c0 + 2 * cd] * p[:, c0 + 2 * cd:c0 + 3 * cd]


def _proj_in(x, mod, g1, w_in, tile, tiles_per_seq):
    t, d = x.shape
    n_in = w_in.shape[1]
    cd = (n_in - 4 * KV_DIM) // 3
    row = lambda i: (i, 0)
    outs = [jax.ShapeDtypeStruct((t, KV_DIM), F32)] * 4 + [jax.ShapeDtypeStruct((t, cd), F32)] * 2
    out_specs = [pl.BlockSpec((tile, KV_DIM), row)] * 4 + [pl.BlockSpec((tile, cd), row)] * 2
    return pl.pallas_call(
        _proj_kernel,
        grid=(t // tile,),
        in_specs=[pl.BlockSpec((tile, d), row),
                  _mod_spec(mod, 0, tile, tiles_per_seq),
                  _mod_spec(mod, 1, tile, tiles_per_seq),
                  pl.BlockSpec((1, d), lambda i: (0, 0)),
                  pl.BlockSpec((d, n_in), lambda i: (0, 0))],
        out_specs=out_specs,
        out_shape=outs,
        compiler_params=_cparams(("parallel",)),
        name="proj_in",
    )(x, mod, mod, g1, w_in)


def _moba_prompt_kernel(q0_ref, q1_ref, k_ref, v_ref, o_ref, kb_s, vb_s, km4_s, e_s, alibi_s):
    c = pl.program_id(1)
    s_len = k_ref.shape[0]
    nb = s_len // MOBA_BLOCK

    @pl.when(c == 0)
    def _():
        kf = k_ref[...]
        kb_s[...] = kf.astype(BF16)
        vb_s[...] = v_ref[...].astype(BF16)
        km = jnp.sum(kf.reshape(nb, MOBA_BLOCK, KV_DIM), axis=1) * (1.0 / MOBA_BLOCK)
        lane_head = lax.broadcasted_iota(jnp.int32, km.shape, 1) // HEAD_DIM
        parts = [jnp.where(lane_head == hk, km, 0.0) for hk in range(KV_HEADS)]
        parts.append(jnp.zeros((LANES - KV_HEADS * nb, KV_DIM), F32))
        km4_s[...] = jnp.concatenate(parts, axis=0).T
        key_blk = lax.broadcasted_iota(jnp.int32, (LANES, s_len), 1) // MOBA_BLOCK
        row = lax.broadcasted_iota(jnp.int32, (LANES, s_len), 0)
        e_s[...] = ((key_blk == (row & (nb - 1))) & (row < KV_HEADS * nb)).astype(BF16)
        tk = lax.broadcasted_iota(jnp.int32, (1, s_len), 1).astype(F32)
        for h in range(N_HEADS):
            alibi_s[h:h + 1, :] = _alibi_slope(h) * tk

    chunks_per_blk = MOBA_BLOCK // Q_CHUNK
    own = c // chunks_per_blk
    for n_past in range(nb):
        @pl.when(own == n_past)
        def _(n_past=n_past):
            _moba_prompt_chunk(n_past, (c % chunks_per_blk) * Q_CHUNK, q0_ref, q1_ref, o_ref,
                               kb_s, vb_s, km4_s, e_s, alibi_s)


def _block_penalty(q, km4, n_past, nb):
    gate = jnp.dot(q, km4, precision=lax.Precision.HIGHEST, preferred_element_type=F32)
    n_of = lax.broadcasted_iota(jnp.int32, gate.shape, 1) & (nb - 1)
    gate = jnp.where(n_of < n_past, gate, -jnp.inf)
    rank = jnp.zeros(gate.shape, F32)
    for delta in range(1, n_past):
        above = pltpu.roll(gate, LANES - delta, 1)
        rank = rank + jnp.where(above > gate, jnp.where(n_of + delta < nb, 1.0, 0.0), 0.0)
        below = pltpu.roll(gate, delta, 1)
        rank = rank + jnp.where(below >= gate, jnp.where(n_of >= delta, 1.0, 0.0), 0.0)
    return jnp.where(rank < float(MOBA_TOPK), 0.0, NEG_BIG).astype(BF16)


def _moba_prompt_chunk(n_past, q_off, q0_ref, q1_ref, o_ref, kb_s, vb_s, km4_s, e_s, alibi_s):
    nb = kb_s.shape[0] // MOBA_BLOCK
    n_keys = (n_past + 1) * MOBA_BLOCK
    past_keys = n_past * MOBA_BLOCK
    q_pos = q_off + lax.broadcasted_iota(jnp.int32, (Q_CHUNK, MOBA_BLOCK), 0)
    causal = jnp.where(q_pos >= lax.broadcasted_iota(jnp.int32, (Q_CHUNK, MOBA_BLOCK), 1), 0.0, NEG_BIG)
    causal = jnp.concatenate([causal] * KV_HEADS, axis=0)
    lane_head = lax.broadcasted_iota(jnp.int32, (Q_CHUNK, KV_DIM), 1) // HEAD_DIM
    need_gate = n_past > MOBA_TOPK
    if need_gate:
        km4 = km4_s[...]
        lane_grp = lax.broadcasted_iota(jnp.int32, (Q_CHUNK, LANES), 1) // nb
    kb = kb_s[0:n_keys, :]
    vb = vb_s[0:n_keys, :]
    outs = [None] * N_HEADS
    for g, q_ref in enumerate((q0_ref, q1_ref)):
        q = q_ref[...]
        qpad = jnp.concatenate([jnp.where(lane_head == hk, q, 0.0) for hk in range(KV_HEADS)], axis=0)
        s = lax.dot_general(qpad.astype(BF16), kb, NT_DIMS, preferred_element_type=F32)
        st = s + jnp.concatenate(
            [jnp.broadcast_to(alibi_s[hk * Q_GROUP + g:hk * Q_GROUP + g + 1, 0:n_keys], (Q_CHUNK, n_keys))
             for hk in range(KV_HEADS)], axis=0)
        if need_gate:
            pen = _block_penalty(q, km4, n_past, nb)
            pen = jnp.concatenate([jnp.where(lane_grp == hk, pen, jnp.zeros((), BF16))
                                   for hk in range(KV_HEADS)], axis=0)
            bias = jnp.dot(pen, e_s[:, 0:past_keys], preferred_element_type=F32)
            st = jnp.concatenate([st[:, :past_keys] + bias, st[:, past_keys:] + causal], axis=1)
        elif n_past > 0:
            st = jnp.concatenate([st[:, :past_keys], st[:, past_keys:] + causal], axis=1)
        else:
            st = st + causal
        mx = jnp.max(st, axis=1, keepdims=True)
        p = jnp.exp(st - mx)
        l = jnp.sum(p, axis=1, keepdims=True)
        o = jnp.dot(p.astype(BF16), vb, preferred_element_type=F32) / l
        for hk in range(KV_HEADS):
            outs[hk * Q_GROUP + g] = o[hk * Q_CHUNK:(hk + 1) * Q_CHUNK, hk * HEAD_DIM:(hk + 1) * HEAD_DIM]
    o_ref[...] = jnp.concatenate(outs, axis=1)


def _moba_prompt(q0, q1, k, v, batch, seq):
    n_chunks = seq // Q_CHUNK
    nb = seq // MOBA_BLOCK
    assert nb & (nb - 1) == 0 and KV_HEADS * nb <= LANES
    qspec = pl.BlockSpec((Q_CHUNK, KV_DIM), lambda b, c: (b * n_chunks + c, 0))
    kvspec = pl.BlockSpec((seq, KV_DIM), lambda b, c: (b, 0))
    return pl.pallas_call(
        _moba_prompt_kernel,
        grid=(batch, n_chunks),
        in_specs=[qspec, qspec, kvspec, kvspec],
        out_specs=pl.BlockSpec((Q_CHUNK, ATTN_DIM), lambda b, c: (b * n_chunks + c, 0)),
        out_shape=jax.ShapeDtypeStruct((batch * seq, ATTN_DIM), F32),
        scratch_shapes=[pltpu.VMEM((seq, KV_DIM), BF16), pltpu.VMEM((seq, KV_DIM), BF16),
                        pltpu.VMEM((KV_DIM, LANES), F32), pltpu.VMEM((LANES, seq), BF16),
                        pltpu.VMEM((N_HEADS, seq), F32)],
        compiler_params=_cparams(("parallel", "arbitrary")),
        name="moba_prompt",
    )(q0, q1, k, v)


def _moba_decode_kernel(pt_ref, qs_ref, kn_ref, vn_ref, *refs, pps, n_blk, pos0, page):
    del pt_ref
    kp = refs[:pps]
    vp = refs[pps:2 * pps]
    o_ref = refs[2 * pps]
    m_s, l_s, km_s, o_s = refs[2 * pps + 1:]
    step = pl.program_id(1)
    n_rows = KV_HEADS * qs_ref.shape[0]
    sd = qs_ref.shape[0] // Q_GROUP
    ppb = MOBA_BLOCK // page
    bps = pps // ppb

    qs = qs_ref[...]
    lane_head = lax.broadcasted_iota(jnp.int32, qs.shape, 1) // HEAD_DIM
    qpad = jnp.concatenate([jnp.where(lane_head == hk, qs, 0.0) for hk in range(KV_HEADS)], axis=0)
    qpad_b = qpad.astype(BF16)
    row = lax.broadcasted_iota(jnp.int32, (n_rows, 1), 0)
    head = (row // qs.shape[0]) * Q_GROUP + (row % qs.shape[0]) // sd
    qi = row % sd
    slope = jnp.zeros((n_rows, 1), F32)
    for h in range(N_HEADS):
        slope = jnp.where(head == h, _alibi_slope(h), slope)
    lane = lax.broadcasted_iota(jnp.int32, (n_rows, LANES), 1)

    @pl.when(step == 0)
    def _():
        m_s[...] = jnp.zeros_like(m_s)
        l_s[...] = jnp.zeros_like(l_s)
        km_s[...] = jnp.zeros_like(km_s)

    kcol = lax.broadcasted_iota(jnp.int32, (KV_DIM, LANES), 1)
    for n in range(bps):
        blk = step * bps + n
        sts = []
        ksum = jnp.zeros((KV_DIM, 1), F32)
        for i in range(ppb):
            kt = kp[n * ppb + i][...]
            s = jnp.dot(qpad_b, kt.astype(BF16), preferred_element_type=F32)
            tk = blk * MOBA_BLOCK + i * page + lax.broadcasted_iota(jnp.int32, (1, page), 1)
            sts.append(s - slope * ((pos0 + qi) - tk).astype(F32))
            ksum = ksum + jnp.sum(kt, axis=1, keepdims=True)
        mx = sts[0].max(axis=1, keepdims=True)
        for st in sts[1:]:
            mx = jnp.maximum(mx, st.max(axis=1, keepdims=True))
        l = jnp.zeros_like(mx)
        o = jnp.zeros((n_rows, KV_DIM), F32)
        for i, st in enumerate(sts):
            p = jnp.exp(st - mx)
            l = l + jnp.sum(p, axis=1, keepdims=True)
            o = o + lax.dot_general(p.astype(BF16), vp[n * ppb + i][...].astype(BF16), NT_DIMS,
                                    preferred_element_type=F32)
        o_s[blk] = o
        here = lane == blk
        m_s[...] = jnp.where(here, mx, m_s[...])
        l_s[...] = jnp.where(here, l, l_s[...])
        km_s[...] = jnp.where(kcol == blk, ksum * (1.0 / MOBA_BLOCK), km_s[...])

    @pl.when(step == pl.num_programs(1) - 1)
    def _():
        valid = lane < n_blk
        gates = jnp.dot(qpad, km_s[...], precision=lax.Precision.HIGHEST, preferred_element_type=F32)
        work = jnp.where(valid, gates, -jnp.inf)
        sel = jnp.zeros((n_rows, LANES), F32)
        for _ in range(min(MOBA_TOPK, n_blk)):
            top = jnp.max(work, axis=1, keepdims=True)
            first = jnp.min(jnp.where(work == top, lane, LANES), axis=1, keepdims=True)
            hit = (lane == first) & valid
            sel = jnp.where(hit, 1.0, sel)
            work = jnp.where(hit, -jnp.inf, work)
        picked = sel > 0.0
        own_scores = []
        for j in range(sd):
            sj = jnp.sum(qpad * kn_ref[j:j + 1, :], axis=1, keepdims=True)
            dj = (qi - j).astype(F32)
            own_scores.append(jnp.where(dj >= 0.0, sj - slope * dj, NEG_BIG))
        m_tot = jnp.max(jnp.where(picked, m_s[...], NEG_BIG), axis=1, keepdims=True)
        for sj in own_scores:
            m_tot = jnp.maximum(m_tot, sj)
        wn = jnp.where(picked, jnp.exp(m_s[...] - m_tot), 0.0)
        denom = jnp.sum(wn * l_s[...], axis=1, keepdims=True)
        acc = jnp.zeros((n_rows, KV_DIM), F32)
        for j, sj in enumerate(own_scores):
            pj = jnp.exp(sj - m_tot)
            denom = denom + pj
            acc = acc + pj * vn_ref[j:j + 1, :]
        for n in range(n_blk):
            coef = jnp.sum(jnp.where(lane == n, wn, 0.0), axis=1, keepdims=True)
            acc = acc + coef * o_s[n]
        o_ref[...] = acc / denom


def _moba_decode(qs, k_new, v_new, cache_k, cache_v, page_table, layer):
    bd, rows8, _ = qs.shape
    sd = k_new.shape[1]
    n_pages = page_table.shape[1]
    page = cache_k.shape[2]
    n_pool = cache_k.shape[1]
    ppb = MOBA_BLOCK // page
    n_blk = n_pages // ppb
    assert n_pages % ppb == 0 and n_blk <= LANES
    pps = min(16, n_pages)
    assert n_pages % pps == 0 and pps % ppb == 0
    ck = cache_k.transpose(0, 1, 3, 4, 2).reshape(cache_k.shape[0], n_pool, KV_DIM, page)
    cv = cache_v.transpose(0, 1, 3, 4, 2).reshape(cache_v.shape[0], n_pool, KV_DIM, page)
    n_rows = KV_HEADS * rows8

    def page_spec(i):
        return pl.BlockSpec((None, None, KV_DIM, page),
                            lambda b, s, pt: (layer, pt[b, s * pps + i], 0, 0))

    seq_spec = lambda r: pl.BlockSpec((None, r, KV_DIM), lambda b, s, pt: (b, 0, 0))
    grid_spec = pltpu.PrefetchScalarGridSpec(
        num_scalar_prefetch=1,
        grid=(bd, n_pages // pps),
        in_specs=[seq_spec(rows8), seq_spec(sd), seq_spec(sd)]
        + [page_spec(i) for i in range(pps)] * 2,
        out_specs=seq_spec(n_rows),
        scratch_shapes=[pltpu.VMEM((n_rows, LANES), F32)] * 2
        + [pltpu.VMEM((KV_DIM, LANES), F32), pltpu.VMEM((n_blk, n_rows, KV_DIM), F32)],
    )
    kern = functools.partial(_moba_decode_kernel, pps=pps, n_blk=n_blk,
                             pos0=n_pages * page, page=page)
    return pl.pallas_call(
        kern,
        grid_spec=grid_spec,
        out_shape=jax.ShapeDtypeStruct((bd, n_rows, KV_DIM), F32),
        compiler_params=_cparams(("parallel", "arbitrary")),
        name="moba_decode",
    )(page_table, qs, k_new, v_new, *([ck] * pps), *([cv] * pps))


def _mix_kernel(*refs, decode, tiles_per_seq):
    if decode:
        (x_ref, attn_ref, gb_ref, u_ref, u1_ref, u2_ref, ga1_ref, sh2_ref, sc2_ref, cw_ref, ga_ref,
         gc_ref, wo_ref, g2_ref, wq_ref, sk_ref, xo_ref, h2_ref, st_ref) = refs
        u0 = u_ref[...]
        u1 = u1_ref[...]
        u2 = u2_ref[...]
    else:
        (x_ref, attn_ref, gb_ref, u_ref, halo_ref, ga1_ref, sh2_ref, sc2_ref, cw_ref, ga_ref,
         gc_ref, wo_ref, g2_ref, wq_ref, sk_ref, xo_ref, h2_ref, st_ref) = refs
        u0 = u_ref[...]
        first = pl.program_id(0) % tiles_per_seq == 0
        halo = jnp.where(first, 0.0, halo_ref[...])
        hr = halo.shape[0]
        rowi = lax.broadcasted_iota(jnp.int32, u0.shape, 0)
        u1 = jnp.where(rowi < 1, halo[hr - 1:hr, :], pltpu.roll(u0, 1, 0))
        u2 = jnp.where(rowi < 1, halo[hr - 2:hr - 1, :],
                       jnp.where(rowi < 2, halo[hr - 1:hr, :], pltpu.roll(u0, 2, 0)))
    cw = cw_ref[...]
    y = cw[0:1, :] * u2
    y = y + cw[1:2, :] * u1
    y = y + cw[2:3, :] * u0
    yc = gb_ref[...] * y
    cat = jnp.concatenate([_rms(attn_ref[...], ga_ref[...]), _rms(yc, gc_ref[...])], axis=1)
    mixed = jnp.dot(cat.astype(BF16), wo_ref[...], preferred_element_type=F32)
    x_new = x_ref[...] + ga1_ref[...] * mixed
    xo_ref[...] = x_new
    h2 = _rms(x_new, g2_ref[...]) * (1.0 + sc2_ref[...]) + sh2_ref[...]
    h_hi = h2.astype(BF16)
    h2_ref[...] = h_hi
    h_lo = (h2 - h_hi.astype(F32)).astype(BF16)
    q = jnp.dot(h_hi, wq_ref[0], preferred_element_type=F32) + (
        jnp.dot(h_hi, wq_ref[1], preferred_element_type=F32)
        + jnp.dot(h_lo, wq_ref[0], preferred_element_type=F32))
    qt = q.T
    qt_hi = qt.astype(BF16)
    qt_lo = (qt - qt_hi.astype(F32)).astype(BF16)
    for h in range(PEER_HEADS):
        for p in range(2):
            rows = slice((h * 2 + p) * PEER_HALF, (h * 2 + p + 1) * PEER_HALF)
            st_ref[h, p] = jnp.dot(sk_ref[0, p], qt_hi[rows, :], preferred_element_type=F32) + (
                jnp.dot(sk_ref[0, p], qt_lo[rows, :], preferred_element_type=F32)
                + jnp.dot(sk_ref[1, p], qt_hi[rows, :], preferred_element_type=F32))


def _mix(x, attn, gb, u, shifted, mod, conv_w, g_attn, g_conv, w_o, g2, w_query, sub_keys,
         tile, tiles_per_seq):
    t, d = x.shape
    cd = u.shape[1]
    decode = shifted is not None
    row = lambda i: (i, 0)
    full2 = lambda a: pl.BlockSpec(a.shape, lambda i: (0,) * a.ndim)
    if decode:
        conv_in = [u, shifted[0], shifted[1]]
        conv_specs = [pl.BlockSpec((tile, cd), row)] * 3
    else:
        halo_rows = 8
        conv_in = [u, u]
        conv_specs = [pl.BlockSpec((tile, cd), row),
                      pl.BlockSpec((halo_rows, cd),
                                   lambda i: (jnp.maximum(i * (tile // halo_rows) - 1, 0), 0))]
    kern = functools.partial(_mix_kernel, decode=decode, tiles_per_seq=tiles_per_seq)
    return pl.pallas_call(
        kern,
        grid=(t // tile,),
        in_specs=[pl.BlockSpec((tile, d), row), pl.BlockSpec((tile, attn.shape[1]), row),
                  pl.BlockSpec((tile, cd), row)] + conv_specs
        + [_mod_spec(mod, 2, tile, tiles_per_seq), _mod_spec(mod, 3, tile, tiles_per_seq),
           _mod_spec(mod, 4, tile, tiles_per_seq),
           full2(conv_w), full2(g_attn), full2(g_conv), full2(w_o), full2(g2), full2(w_query),
           full2(sub_keys)],
        out_specs=[pl.BlockSpec((tile, d), row), pl.BlockSpec((tile, d), row),
                   pl.BlockSpec((PEER_HEADS, 2, PEER_KEYS, tile), lambda i: (0, 0, 0, i))],
        out_shape=[jax.ShapeDtypeStruct((t, d), F32), jax.ShapeDtypeStruct((t, d), BF16),
                   jax.ShapeDtypeStruct((PEER_HEADS, 2, PEER_KEYS, t), F32)],
        compiler_params=_cparams(("parallel",)),
        name="mix_scores",
    )(x, attn, gb, *conv_in, mod, mod, mod, conv_w, g_attn, g_conv, w_o, g2, w_query, sub_keys)


def _top_rows(work, n_take, vals_ref):
    n = work.shape[0]
    idx = lax.broadcasted_iota(jnp.int32, work.shape, 0).astype(F32)
    rank = jnp.full(work.shape, float(n_take), F32)
    for r in range(n_take):
        top = jnp.max(work, axis=0, keepdims=True)
        first = jnp.min(jnp.where(work == top, idx, float(n)), axis=0, keepdims=True)
        hit = idx == first
        rank = jnp.where(hit, float(r), rank)
        work = jnp.where(hit, -jnp.inf, work)
        if vals_ref is not None:
            vals_ref[r:r + 1, :] = top
    return rank


def _top_rows_distinct(work, n_take, vals_ref, want_rank=True):
    rank = jnp.full(work.shape, float(n_take), F32) if want_rank else None
    for r in range(n_take):
        top = jnp.max(work, axis=0, keepdims=True)
        hit = work == top
        if want_rank:
            rank = jnp.where(hit, float(r), rank)
        work = jnp.where(hit, -jnp.inf, work)
        if vals_ref is not None:
            vals_ref[r:r + 1, :] = top
    taken = rank < float(n_take) if want_rank else work == -jnp.inf
    n_marked = jnp.sum(jnp.where(taken, 1.0, 0.0), axis=0, keepdims=True)
    return rank, n_marked


def _dup_bf16(x):
    u = lax.bitcast_convert_type(x, jnp.uint32)
    r = (u + jnp.uint32(0x7FFF) + ((u >> 16) & jnp.uint32(1))) >> 16
    return r | (r << 16)


def _write_tables(h, refs, s1, s2, rank2, c1, z, a0, b0):
    c1_ref, e1_ref, r2_ref, e2_ref = refs
    c1_ref[h] = _dup_bf16(c1)
    e1_ref[h] = _dup_bf16(jnp.exp(s1 - a0) / z)
    r2_ref[h] = pltpu.bitcast(rank2.astype(BF16), jnp.uint32)
    e2_ref[h] = pltpu.bitcast(jnp.exp(s2 - b0).astype(BF16), jnp.uint32)


def _route_head_exact(h, s_ref, refs, a_s, b_s):
    s1 = s_ref[h, 0]
    s2 = s_ref[h, 1]
    rank1 = _top_rows(s1, PEER_TOPK, a_s)
    rank2 = _top_rows(s2, PEER_TOPK, b_s)
    a = a_s[...]
    b = b_s[...]
    cand = jnp.concatenate([a[r:r + 1, :] + b for r in range(PEER_TOPK)], axis=0)
    crank = _top_rows(cand, PEER_TOPK, None)
    sel = crank < float(PEER_TOPK)
    top0 = a[0:1, :] + b[0:1, :]
    z = jnp.sum(jnp.where(sel, jnp.exp(cand - top0), 0.0), axis=0, keepdims=True)
    self = jnp.where(sel, 1.0, 0.0)
    c1 = jnp.zeros_like(s1)
    for r in range(PEER_TOPK):
        cnt = jnp.sum(self[r * PEER_TOPK:(r + 1) * PEER_TOPK, :], axis=0, keepdims=True)
        c1 = jnp.where(rank1 == float(r), cnt, c1)
    _write_tables(h, refs, s1, s2, rank2, c1, z, a[0:1, :], b[0:1, :])


def _route_head_distinct(h, s_ref, refs, a_s, b_s):
    s1 = s_ref[h, 0]
    s2 = s_ref[h, 1]
    _, n1 = _top_rows_distinct(s1, PEER_TOPK, a_s, want_rank=False)
    rank2, n2 = _top_rows_distinct(s2, PEER_TOPK, b_s)
    a = a_s[...]
    b = b_s[...]
    half = PEER_TOPK // 2
    cand = jnp.concatenate([a[0:1, :] + b] + [a[r:r + 1, :] + b[0:half, :] for r in range(1, half)]
                           + [a[half:, :] + b[0:1, :]], axis=0)
    row = lax.broadcasted_iota(jnp.int32, cand.shape, 0)
    assert half == 8
    r_mid = ((row - PEER_TOPK) >> 3) + 1
    c_mid = (row - PEER_TOPK) & 7
    in_mid = (row >= PEER_TOPK) & (row < PEER_TOPK + half * (half - 1))
    cand = jnp.where(in_mid & ((r_mid + 1) * (c_mid + 1) > PEER_TOPK), -jnp.inf, cand)
    crank, nc = _top_rows_distinct(cand, PEER_TOPK, None)
    sel = crank < float(PEER_TOPK)
    top0 = a[0:1, :] + b[0:1, :]
    z = jnp.sum(jnp.where(sel, jnp.exp(cand - top0), 0.0), axis=0, keepdims=True)
    self = jnp.where(sel, 1.0, 0.0)
    cnts = [jnp.sum(self[0:PEER_TOPK, :], axis=0, keepdims=True)]
    for r in range(1, half):
        lo = PEER_TOPK + (r - 1) * half
        cnts.append(jnp.sum(self[lo:lo + half, :], axis=0, keepdims=True))
    tail = PEER_TOPK + half * (half - 1)
    cnts += [self[tail + r:tail + r + 1, :] for r in range(half)]
    c1 = jnp.zeros_like(s1)
    for r in range(PEER_TOPK):
        c1 = jnp.where(s1 == a[r:r + 1, :], cnts[r], c1)
    _write_tables(h, refs, s1, s2, rank2, c1, z, a[0:1, :], b[0:1, :])
    want = float(PEER_TOPK)
    return jnp.abs(n1 - want) + jnp.abs(n2 - want) + jnp.abs(nc - want)


def _route_kernel(s_ref, c1_ref, e1_ref, r2_ref, e2_ref, a_s, b_s):
    refs = (c1_ref, e1_ref, r2_ref, e2_ref)

    def head_pair(i, carry):
        heads = (2 * i, 2 * i + 1)
        ties = [_route_head_distinct(h, s_ref, refs, a_s.at[k], b_s.at[k]) for k, h in enumerate(heads)]
        for k, h in enumerate(heads):
            @pl.when(jnp.max(ties[k]) > 0.0)
            def _(k=k, h=h):
                _route_head_exact(h, s_ref, refs, a_s.at[k], b_s.at[k])

        return carry

    lax.fori_loop(0, PEER_HEADS // 2, head_pair, 0)


def _route(scores):
    _, _, nk, t = scores.shape
    tile = LANES
    wspec = pl.BlockSpec((PEER_HEADS, nk, tile), lambda i: (0, 0, i))
    pspec = pl.BlockSpec((PEER_HEADS, nk // 2, tile), lambda i: (0, 0, i))
    words = jax.ShapeDtypeStruct((PEER_HEADS, nk, t), jnp.uint32)
    packed = jax.ShapeDtypeStruct((PEER_HEADS, nk // 2, t), jnp.uint32)
    return pl.pallas_call(
        _route_kernel,
        grid=(t // tile,),
        in_specs=[pl.BlockSpec((PEER_HEADS, 2, nk, tile), lambda i: (0, 0, 0, i))],
        out_specs=[wspec, wspec, pspec, pspec],
        out_shape=[words, words, packed, packed],
        scratch_shapes=[pltpu.VMEM((2, PEER_TOPK, tile), F32)] * 2,
        compiler_params=_cparams(("parallel",)),
        name="peer_route",
    )(scores)


def _gelu(x):
    return 0.5 * x * (1.0 + lax.erf(x * (2.0 ** -0.5)))


def _peer_kernel(h2_ref, u_ref, vt_ref, c1_ref, e1_ref, r2_ref, e2_ref, x_ref, ga2_ref, gf_ref,
                 o_ref, acc_s, w_a, w_b, ht_a, ht_b, *, final):
    j = pl.program_id(1)
    n_e = pl.num_programs(1) - 1
    slot = j % 2

    @pl.when(j == 0)
    def _():
        acc_s[...] = jnp.zeros_like(acc_s)

    def expert_tile(w_prev, w_cur, first=False, last=False):
        xb = h2_ref[...]
        tt = xb.shape[0]
        grp = PEER_KEYS // BF16_ROWS
        pair = 2 * PEER_KEYS
        n_stages = u_ref.shape[0] // pair
        v_rows = vt_ref.shape[0] // n_stages
        hts = (ht_a, ht_b)

        def scores(k):
            rows = slice(k * pair, (k + 1) * pair)
            hts[k % 2][...] = lax.dot_general(u_ref[rows, :], xb, NT_DIMS, preferred_element_type=F32)

        def second_product(k):
            rows = slice(k * v_rows, (k + 1) * v_rows)
            acc_s[rows, :] += jnp.dot(vt_ref[rows, :], w_prev[...], preferred_element_type=F32)

        def activations(k):
            ht_ref = hts[k % 2]
            c1_rows = [[pltpu.bitcast(jnp.broadcast_to(c1_ref[h, 2 * k + sub:2 * k + sub + 1, :], (8, tt)), BF16)
                        for h in range(PEER_HEADS)] for sub in range(2)]
            e1_rows = [[pltpu.bitcast(jnp.broadcast_to(e1_ref[h, 2 * k + sub:2 * k + sub + 1, :], (8, tt)), BF16)
                        for h in range(PEER_HEADS)] for sub in range(2)]
            for cb in range(tt // LANES):
                cols = slice(cb * LANES, (cb + 1) * LANES)
                gates = [jnp.zeros((PEER_KEYS, LANES), BF16) for _ in range(2)]
                for h in range(PEER_HEADS):
                    r2 = pltpu.bitcast(r2_ref[h, :, cols], BF16)
                    e2 = pltpu.bitcast(e2_ref[h, :, cols], BF16)
                    for sub in range(2):
                        c1 = jnp.concatenate([c1_rows[sub][h][:, cols]] * grp, axis=0)
                        e1 = jnp.concatenate([e1_rows[sub][h][:, cols]] * grp, axis=0)
                        gates[sub] = gates[sub] + jnp.where(r2 < c1, e2, jnp.zeros((), BF16)) * e1
                for sub in range(2):
                    r0 = k * pair + sub * PEER_KEYS
                    act = _gelu(ht_ref[sub * PEER_KEYS:(sub + 1) * PEER_KEYS, cols]).astype(BF16)
                    w_cur[r0:r0 + PEER_KEYS, cols] = act * gates[sub]

        if last:
            for k in range(n_stages):
                second_product(k)
            return
        scores(0)
        for k in range(1, n_stages):
            activations(k - 1)
            scores(k)
            if not first:
                second_product(k - 1)
        activations(n_stages - 1)
        if not first:
            second_product(n_stages - 1)

    @pl.when(j == 0)
    def _():
        expert_tile(None, w_a, first=True)

    middle = (j > 0) & (j < n_e)

    @pl.when(middle & (slot == 0))
    def _():
        expert_tile(w_b, w_a)

    @pl.when(middle & (slot == 1))
    def _():
        expert_tile(w_a, w_b)

    @pl.when((j == n_e) & (slot == 0))
    def _():
        expert_tile(w_b, None, last=True)

    @pl.when((j == n_e) & (slot == 1))
    def _():
        expert_tile(w_a, None, last=True)

    @pl.when(j == n_e)
    def _():
        out = x_ref[...] + ga2_ref[...] * acc_s[...].T
        if final:
            out = _rms(out, gf_ref[...])
        o_ref[...] = out


def _peer(h2, u_tab, vt_tab, tabs, x_new, mod, g_final, tile, tiles_per_seq, final):
    t, d = x_new.shape
    n_exp = u_tab.shape[0]
    e_tile = min(PEER_EXPERT_TILE, n_exp)
    n_stages = e_tile // (2 * PEER_KEYS)
    assert n_exp % e_tile == 0 and e_tile % (2 * PEER_KEYS) == 0 and d % (n_stages * BF16_ROWS) == 0
    n_e = n_exp // e_tile
    c1, e1, r2, e2 = tabs
    nk = c1.shape[1]
    row = lambda i, j: (i, 0)
    cur = lambda j: jnp.minimum(j, n_e - 1)
    i1_spec = pl.BlockSpec((PEER_HEADS, e_tile // nk, tile), lambda i, j: (0, cur(j), i))
    i2_spec = pl.BlockSpec((PEER_HEADS, nk // 2, tile), lambda i, j: (0, 0, i))
    return pl.pallas_call(
        functools.partial(_peer_kernel, final=final),
        grid=(t // tile, n_e + 1),
        in_specs=[pl.BlockSpec((tile, d), row),
                  pl.BlockSpec((e_tile, d), lambda i, j: (cur(j), 0)),
                  pl.BlockSpec((d, e_tile), lambda i, j: (0, jnp.maximum(j - 1, 0))),
                  i1_spec, i1_spec, i2_spec, i2_spec,
                  pl.BlockSpec((tile, d), row),
                  _mod_spec(mod, 5, tile, tiles_per_seq),
                  pl.BlockSpec((1, d), lambda i, j: (0, 0))],
        out_specs=pl.BlockSpec((tile, d), row),
        out_shape=jax.ShapeDtypeStruct((t, d), F32),
        scratch_shapes=[pltpu.VMEM((d, tile), F32),
                        pltpu.VMEM((e_tile, tile), BF16), pltpu.VMEM((e_tile, tile), BF16),
                        pltpu.VMEM((2 * PEER_KEYS, tile), F32), pltpu.VMEM((2 * PEER_KEYS, tile), F32)],
        compiler_params=_cparams(("parallel", "arbitrary")),
        name="peer_dense",
    )(h2, u_tab, vt_tab, c1, e1, r2, e2, x_new, mod, g_final)


def _split_bf16(w):
    hi = w.astype(BF16)
    return jnp.stack([hi, (w - hi.astype(F32)).astype(BF16)])


def _token_tile(seq, want):
    tile = min(want, seq)
    assert seq % tile == 0
    return tile


def _layer(x, mod, past, conv_prev, weights, batch, seq, layer, final):
    (g1, w_in, conv_w, g_attn, g_conv, w_o, g2, w_query, sub_keys, u_tab, vt_tab, g_final) = weights
    decode = past is not None
    t, d = x.shape
    if decode:
        tile, tps = t, 1
    else:
        tile = _token_tile(seq, 512)
        tps = seq // tile
    q0, q1, k, v, gb, u = _proj_in(x, mod, g1, w_in, tile, tps)
    cd = u.shape[1]
    if decode:
        cache_k, cache_v, page_table = past
        qs = jnp.concatenate([q0.reshape(batch, seq, KV_DIM), q1.reshape(batch, seq, KV_DIM)], axis=1)
        o = _moba_decode(qs, k.reshape(batch, seq, KV_DIM), v.reshape(batch, seq, KV_DIM),
                         cache_k, cache_v, page_table, layer)
        o = o.reshape(batch, KV_HEADS, Q_GROUP, seq, KV_HEADS, HEAD_DIM)
        o = jnp.stack([o[:, hk, :, :, hk, :] for hk in range(KV_HEADS)], axis=1)
        attn = o.transpose(0, 3, 1, 2, 4).reshape(t, ATTN_DIM)
        xp = jnp.concatenate([conv_prev, u.reshape(batch, seq, cd)], axis=1)
        shifted = (xp[:, 1:1 + seq].reshape(t, cd), xp[:, 0:seq].reshape(t, cd))
        conv_new = xp[:, -(CONV_WIDTH - 1):]
    else:
        attn = _moba_prompt(q0, q1, k, v, batch, seq)
        shifted = None
        conv_new = u.reshape(batch, seq, cd)[:, -(CONV_WIDTH - 1):]
    x_new, h2, scores = _mix(x, attn, gb, u, shifted, mod, conv_w, g_attn, g_conv, w_o, g2,
                             w_query, sub_keys, tile, tps)
    tabs = _route(scores)
    x_out = _peer(h2, u_tab, vt_tab, tabs, x_new, mod, g_final, tile, tps, final)
    return x_out, k, v, conv_new


def kernel(x_prompt, x_sample, cache_k, cache_v, state_conv, page_table, c_prompt, c_sample,
           w_ada, b_ada, g_norm1, g_norm2, w_in, conv_w, g_out_attn, g_out_conv, w_o,
           w_query, sub_keys, u_experts, v_experts, g_final):
    b, s, d = x_prompt.shape
    bd, sd, _ = x_sample.shape
    depth = w_ada.shape[0]
    assert s % MOBA_BLOCK == 0 and (bd * sd) % LANES == 0

    mod = _ada_mod(jnp.concatenate([c_prompt, c_sample], axis=0), w_ada, b_ada)

    scale = HEAD_DIM ** -0.5
    cols0 = jnp.concatenate([jnp.arange(HEAD_DIM) + (hk * Q_GROUP) * HEAD_DIM for hk in range(KV_HEADS)])
    cols1 = cols0 + HEAD_DIM

    xp = x_prompt.reshape(b * s, d)
    xs = x_sample.reshape(bd * sd, d)
    zero_conv = jnp.zeros((b, CONV_WIDTH - 1, conv_w.shape[-1]), x_prompt.dtype)
    ks_p, vs_p, cs_p, ks_s, vs_s, cs_s = [], [], [], [], [], []
    for l in range(depth):
        wl = w_in[l]
        w_in_l = jnp.concatenate([wl[:, cols0] * scale, wl[:, cols1] * scale, wl[:, ATTN_DIM:]],
                                 axis=1).astype(BF16)
        weights = (g_norm1[l][None], w_in_l, conv_w[l], g_out_attn[l][None], g_out_conv[l][None],
                   w_o[l].astype(BF16), g_norm2[l][None], _split_bf16(w_query[l]), _split_bf16(sub_keys[l]),
                   u_experts[l].astype(BF16), v_experts[l].T.astype(BF16), g_final[None])
        final = l == depth - 1
        mod_p = mod[l, :b][:, None, :]
        mod_s = jnp.repeat(mod[l, b:], sd, axis=0)
        xp, k, v, cn = _layer(xp, mod_p, None, zero_conv, weights, b, s, l, final)
        ks_p.append(k.reshape(b, s, KV_HEADS, HEAD_DIM))
        vs_p.append(v.reshape(b, s, KV_HEADS, HEAD_DIM))
        cs_p.append(cn)
        xs, k, v, cn = _layer(xs, mod_s, (cache_k, cache_v, page_table), state_conv[l], weights,
                              bd, sd, l, final)
        ks_s.append(k.reshape(bd, sd, KV_HEADS, HEAD_DIM))
        vs_s.append(v.reshape(bd, sd, KV_HEADS, HEAD_DIM))
        cs_s.append(cn)
    return (xp.reshape(b, s, d), xs.reshape(bd, sd, d),
            jnp.stack(ks_p), jnp.stack(vs_p), jnp.stack(cs_p),
            jnp.stack(ks_s), jnp.stack(vs_s), jnp.stack(cs_s))
```
